```python
import jax
import jax.numpy as jnp
from jax import lax
import numpy as np

D_MODEL = 1024
BATCH = 8
SEQ = 4096
DEPTH = 4

CTX_LEN = 256
GRID_W = 64
N_BRANCH = 3
BRANCH_W = D_MODEL // 2
HEAD_DIM = 64
ML_HEADS = 4
ML_HEAD_DIM = BRANCH_W // ML_HEADS
ML_CHUNK = 64
GQA_HEADS = BRANCH_W // HEAD_DIM
GQA_KV_HEADS = 2
GQA_GROUP = GQA_HEADS // GQA_KV_HEADS
NA_HEADS = BRANCH_W // HEAD_DIM
NA_WIN_R = 8
NA_WIN_C = 16
Q_BLOCK = 128
ROPE_THETA = 10000.0
FFN_DIM = ((8 * D_MODEL // 3 + 127) // 128) * 128
CONV_K = 3
EPS = 1e-6
IN_SIZES = (BRANCH_W, BRANCH_W, BRANCH_W, BRANCH_W, 4 * ML_HEADS,
            GQA_HEADS * HEAD_DIM, GQA_KV_HEADS * HEAD_DIM, GQA_KV_HEADS * HEAD_DIM,
            NA_HEADS * HEAD_DIM, NA_HEADS * HEAD_DIM, NA_HEADS * HEAD_DIM,
            N_BRANCH * D_MODEL)
IN_SPLITS = tuple(sum(IN_SIZES[:i + 1]) for i in range(len(IN_SIZES) - 1))
D_IN = sum(IN_SIZES)

kernel_name = 'hybrid_mlstm_gqa_natten_dit_block'


def rms_norm(x, g):
    xf = x.astype(jnp.float32)
    y = xf * lax.rsqrt(jnp.mean(xf * xf, axis=-1, keepdims=True) + EPS)
    return (y * g.astype(jnp.float32)).astype(x.dtype)


def modulate(h, shift, scale):
    return h * (1 + scale) + shift


def rope_tables(n_tok, dtype):
    half = HEAD_DIM // 2
    t = jnp.arange(n_tok)
    inv = ROPE_THETA ** (-jnp.arange(0, half, 2, dtype=jnp.float32) / half)
    ang_r = (t // GRID_W).astype(jnp.float32)[:, None] * inv
    ang_c = (t % GRID_W).astype(jnp.float32)[:, None] * inv
    ang = jnp.concatenate([ang_r, ang_r, ang_c, ang_c], axis=-1)
    return jnp.cos(ang).astype(dtype), jnp.sin(ang).astype(dtype)


def apply_rope(x, cos, sin):
    x1, x2, x3, x4 = jnp.split(x, 4, axis=-1)
    rot = jnp.concatenate([-x2, x1, -x4, x3], axis=-1)
    return x * cos[:, None, :] + rot * sin[:, None, :]


def split_heads(z, n_heads):
    B, T, _ = z.shape
    return z.reshape(B, T, n_heads, HEAD_DIM)


def attend(q, k, v):
    s = jnp.einsum('bqkgd,bskd->bkgqs', q, k).astype(jnp.float32) * (HEAD_DIM ** -0.5)
    p = jax.nn.softmax(s, axis=-1).astype(v.dtype)
    return jnp.einsum('bkgqs,bskd->bqkgd', p, v)


def zero_state(B):
    return (jnp.zeros((B, ML_HEADS, ML_HEAD_DIM, ML_HEAD_DIM), jnp.float32),
            jnp.zeros((B, ML_HEADS, ML_HEAD_DIM), jnp.float32),
            jnp.zeros((B, ML_HEADS), jnp.float32))


def mlstm_scan(q, k, v, ig, lf, state):
    B, H, T, d = q.shape
    nc = T // ML_CHUNK

    def to_chunks(a):
        return jnp.moveaxis(a.reshape((B, H, nc, ML_CHUNK) + a.shape[3:]), 2, 0)

    tril = jnp.tril(jnp.ones((ML_CHUNK, ML_CHUNK), dtype=bool))

    def body(carry, xs):
        C, n, m = carry
        qc, kc, vc, igc, lfc = xs
        b = jnp.cumsum(lfc, axis=-1)
        logd = jnp.where(tril, b[..., :, None] - b[..., None, :] + igc[..., None, :], -jnp.inf)
        m_t = jnp.maximum(b + m[..., None], jnp.max(logd, axis=-1))
        dw = jnp.exp(logd - m_t[..., None])
        inter = jnp.exp(b + m[..., None] - m_t)
        s = jnp.einsum('bhtd,bhsd->bhts', qc, kc) * dw
        num = jnp.einsum('bhts,bhse->bhte', s, vc) + inter[..., None] * jnp.einsum('bhtd,bhde->bhte', qc, C)
        den = jnp.sum(s, axis=-1) + inter * jnp.einsum('bhtd,bhd->bht', qc, n)
        h = num / jnp.maximum(jnp.abs(den), jnp.exp(-m_t))[..., None]
        b_last = b[..., -1]
        log_w = b_last[..., None] - b + igc
        m_new = jnp.maximum(b_last + m, jnp.max(log_w, axis=-1))
        w = jnp.exp(log_w - m_new[..., None])
        decay = jnp.exp(b_last + m - m_new)
        C_new = decay[..., None, None] * C + jnp.einsum('bhs,bhsd,bhse->bhde', w, kc, vc)
        n_new = decay[..., None] * n + jnp.einsum('bhs,bhsd->bhd', w, kc)
        return (C_new, n_new, m_new), h

    final, h = lax.scan(body, state, (to_chunks(q), to_chunks(k), to_chunks(v), to_chunks(ig), to_chunks(lf)))
    h = jnp.moveaxis(h, 0, 2).reshape(B, H, T, d)
    return h, final


def mlstm_branch(zq, zk, zv, zo, zgate, gate_bias, g_out, state_f, state_b):
    B, T, _ = zq.shape

    def heads(z):
        return z.reshape(B, T, ML_HEADS, ML_HEAD_DIM).transpose(0, 2, 1, 3)

    q, k, v = heads(zq), heads(zk) * (ML_HEAD_DIM ** -0.5), heads(zv)
    g = (zgate.astype(jnp.float32) + gate_bias.astype(jnp.float32)).reshape(B, T, 4, ML_HEADS).transpose(2, 0, 3, 1)
    i_f, f_f, i_b, f_b = g[0], g[1], g[2], g[3]
    h_f, st_f = mlstm_scan(q, k, v, i_f, jax.nn.log_sigmoid(f_f), state_f)
    rev = lambda a: jnp.flip(a, axis=2)
    h_b, st_b = mlstm_scan(rev(q), rev(k), rev(v), rev(i_b), jax.nn.log_sigmoid(rev(f_b)), state_b)
    h = (h_f + rev(h_b)).transpose(0, 2, 1, 3)
    h = rms_norm(h, g_out.reshape(ML_HEADS, ML_HEAD_DIM)).reshape(B, T, BRANCH_W)
    y = h * jax.nn.sigmoid(zo.astype(jnp.float32))
    return y.astype(zq.dtype), st_f, st_b


def gqa_qkv(zq, zk, zv, g_q, g_k):
    B, T, _ = zq.shape
    q = rms_norm(zq.reshape(B, T, GQA_HEADS, HEAD_DIM), g_q)
    k = rms_norm(zk.reshape(B, T, GQA_KV_HEADS, HEAD_DIM), g_k)
    v = zv.reshape(B, T, GQA_KV_HEADS, HEAD_DIM)
    return q, k, v


def gqa_latent(q, k, v, kc, vc):
    B, T = q.shape[:2]
    k_all = jnp.concatenate([kc, k], axis=1)
    v_all = jnp.concatenate([vc, v], axis=1)
    qb = q.reshape(B, T // Q_BLOCK, Q_BLOCK, GQA_KV_HEADS, GQA_GROUP, HEAD_DIM)
    out = lax.map(lambda qi: attend(qi, k_all, v_all), jnp.moveaxis(qb, 1, 0))
    return jnp.moveaxis(out, 0, 1).reshape(B, T, BRANCH_W)


def na_latent(q, k, v, kc, vc, rpb):
    B, T, H, d = q.shape
    rows = T // GRID_W
    wr = min(NA_WIN_R, rows)
    wc = NA_WIN_C
    qg = q.reshape(B, rows, GRID_W, H, d)
    kg = k.reshape(B, rows, GRID_W, H, d)
    vg = v.reshape(B, rows, GRID_W, H, d)
    cols = jnp.arange(GRID_W)
    c0 = jnp.clip(cols - wc // 2, 0, GRID_W - wc)
    col_idx = c0[:, None] + jnp.arange(wc)[None, :]
    dc = col_idx - cols[:, None] + (NA_WIN_C - 1)
    scale = d ** -0.5

    def row_block(r):
        r0 = jnp.clip(r - wr // 2, 0, rows - wr)
        k_win = lax.dynamic_slice_in_dim(kg, r0, wr, axis=1)[:, :, col_idx]
        v_win = lax.dynamic_slice_in_dim(vg, r0, wr, axis=1)[:, :, col_idx]
        q_row = lax.dynamic_index_in_dim(qg, r, axis=1, keepdims=False)
        dr = r0 + jnp.arange(wr) - r + (NA_WIN_R - 1)
        bias = rpb[:, dr[None, :, None], dc[:, None, :]]
        s_win = jnp.einsum('bqhd,brqjhd->bhqrj', q_row, k_win).astype(jnp.float32) * scale + bias.astype(jnp.float32)
        s_ctx = jnp.einsum('bqhd,bshd->bhqs', q_row, kc).astype(jnp.float32) * scale
        s = jnp.concatenate([s_win.reshape(B, H, GRID_W, wr * wc), s_ctx], axis=-1)
        p = jax.nn.softmax(s, axis=-1).astype(v.dtype)
        p_win = p[..., :wr * wc].reshape(B, H, GRID_W, wr, wc)
        p_ctx = p[..., wr * wc:]
        o = jnp.einsum('bhqrj,brqjhd->bqhd', p_win, v_win) + jnp.einsum('bhqs,bshd->bqhd', p_ctx, vc)
        return o.reshape(B, GRID_W, H * d)

    out = lax.map(row_block, jnp.arange(rows))
    return jnp.moveaxis(out, 0, 1).reshape(B, T, H * d)


def merge_branches(y_ml, y_ga, y_na, zg, w_branch, w_out):
    B, T, _ = zg.shape
    gates = jax.nn.sigmoid(zg.astype(jnp.float32)).astype(zg.dtype).reshape(B, T, N_BRANCH, D_MODEL)
    ys = jnp.stack([y_ml, y_ga, y_na], axis=2)
    proj = jnp.einsum('btnc,ncd->btnd', ys, w_branch)
    return jnp.sum(gates * proj, axis=2) @ w_out


def conv_ffn(h, w_up, conv_w, conv_b, w_down):
    u = h @ w_up
    up = jnp.pad(u, ((0, 0), (1, 1), (0, 0)))
    u = up[:, :-2] * conv_w[0] + up[:, 1:-1] * conv_w[1] + up[:, 2:] * conv_w[2] + conv_b
    a, g = jnp.split(u, 2, axis=-1)
    return (a * jax.nn.silu(g)) @ w_down


def setup_inputs(seed: int = 0) -> dict:
    key = jax.random.key(seed)
    ks = jax.random.split(key, 24)
    f32 = jnp.float32

    def nrm(k, shape, s):
        return jax.random.normal(k, shape, f32) * s

    x = nrm(ks[0], (BATCH, SEQ, D_MODEL), 1.0)
    c = nrm(ks[1], (BATCH, D_MODEL), 1.0)
    ctx = nrm(ks[2], (BATCH, CTX_LEN, D_MODEL), 1.0)
    c_ctx = nrm(ks[3], (D_MODEL,), 1.0)
    w_mod = nrm(ks[4], (DEPTH, D_MODEL, 6 * D_MODEL), 0.5 * D_MODEL ** -0.5)
    b_mod = nrm(ks[5], (DEPTH, 6 * D_MODEL), 0.02)
    g_pre_mix = 1.0 + nrm(ks[6], (DEPTH, D_MODEL), 0.05)
    g_post_mix = 1.0 + nrm(ks[7], (DEPTH, D_MODEL), 0.05)
    g_pre_ffn = 1.0 + nrm(ks[8], (DEPTH, D_MODEL), 0.05)
    g_post_ffn = 1.0 + nrm(ks[9], (DEPTH, D_MODEL), 0.05)
    w_in = nrm(ks[10], (DEPTH, D_MODEL, D_IN), D_MODEL ** -0.5)
    i_bias = nrm(ks[11], (DEPTH, 2, ML_HEADS), 0.1)
    f_bias = jnp.linspace(3.0, 6.0, ML_HEADS, dtype=f32) + nrm(ks[12], (DEPTH, 2, ML_HEADS), 0.1)
    b_ml_gates = jnp.stack([i_bias[:, 0], f_bias[:, 0], i_bias[:, 1], f_bias[:, 1]], axis=1).reshape(DEPTH, 4 * ML_HEADS)
    g_ml_out = 1.0 + nrm(ks[13], (DEPTH, BRANCH_W), 0.05)
    g_q = 1.0 + nrm(ks[14], (DEPTH, HEAD_DIM), 0.05)
    g_k = 1.0 + nrm(ks[15], (DEPTH, HEAD_DIM), 0.05)
    na_rpb = nrm(ks[16], (DEPTH, NA_HEADS, 2 * NA_WIN_R - 1, 2 * NA_WIN_C - 1), 0.1)
    w_branch = nrm(ks[17], (DEPTH, N_BRANCH, BRANCH_W, D_MODEL), BRANCH_W ** -0.5)
    w_out = nrm(ks[18], (DEPTH, D_MODEL, D_MODEL), D_MODEL ** -0.5)
    w_up = nrm(ks[19], (DEPTH, D_MODEL, 2 * FFN_DIM), D_MODEL ** -0.5)
    conv_w = nrm(ks[20], (DEPTH, CONV_K, 2 * FFN_DIM), 0.3) + jnp.array([0.0, 1.0, 0.0], f32)[None, :, None]
    conv_b = nrm(ks[21], (DEPTH, 2 * FFN_DIM), 0.02)
    w_down = nrm(ks[22], (DEPTH, FFN_DIM, D_MODEL), FFN_DIM ** -0.5)
    return {'x': x, 'c': c, 'ctx': ctx, 'c_ctx': c_ctx, 'w_mod': w_mod, 'b_mod': b_mod,
            'g_pre_mix': g_pre_mix, 'g_post_mix': g_post_mix, 'g_pre_ffn': g_pre_ffn, 'g_post_ffn': g_post_ffn,
            'w_in': w_in, 'b_ml_gates': b_ml_gates, 'g_ml_out': g_ml_out, 'g_q': g_q, 'g_k': g_k,
            'na_rpb': na_rpb, 'w_branch': w_branch, 'w_out': w_out, 'w_up': w_up, 'conv_w': conv_w,
            'conv_b': conv_b, 'w_down': w_down}


def reference(x, c, ctx, c_ctx, w_mod, b_mod, g_pre_mix, g_post_mix, g_pre_ffn, g_post_ffn, w_in, b_ml_gates,
              g_ml_out, g_q, g_k, na_rpb, w_branch, w_out, w_up, conv_w, conv_b, w_down):
    B, T, _ = x.shape
    cos, sin = rope_tables(T, x.dtype)
    xc = ctx
    for l in range(DEPTH):
        last = l == DEPTH - 1
        mod = (jax.nn.silu(c) @ w_mod[l] + b_mod[l])[:, None, :]
        mod_c = jax.nn.silu(c_ctx) @ w_mod[l] + b_mod[l]
        sh1, sc1, ga1, sh2, sc2, ga2 = jnp.split(mod, 6, axis=-1)
        csh1, csc1, cga1, csh2, csc2, cga2 = jnp.split(mod_c, 6, axis=-1)

        h = modulate(rms_norm(x, g_pre_mix[l]), sh1, sc1)
        hc = modulate(rms_norm(xc, g_pre_mix[l]), csh1, csc1)
        (mq, mk, mv, mo, mg, aq, ak, av, nq, nk, nv, zg) = jnp.split(h @ w_in[l], IN_SPLITS, axis=-1)
        (cmq, cmk, cmv, cmo, cmg, caq, cak, cav, cnq, cnk, cnv, czg) = jnp.split(hc @ w_in[l], IN_SPLITS, axis=-1)

        yc_ml, st_f, st_b = mlstm_branch(cmq, cmk, cmv, cmo, cmg, b_ml_gates[l], g_ml_out[l], zero_state(B), zero_state(B))
        y_ml, _, _ = mlstm_branch(mq, mk, mv, mo, mg, b_ml_gates[l], g_ml_out[l], st_f, st_b)

        qc_a, kc_a, vc_a = gqa_qkv(caq, cak, cav, g_q[l], g_k[l])
        q_a, k_a, v_a = gqa_qkv(aq, ak, av, g_q[l], g_k[l])
        y_ga = gqa_latent(apply_rope(q_a, cos, sin), apply_rope(k_a, cos, sin), v_a, kc_a, vc_a)

        kc_n = split_heads(cnk, NA_HEADS)
        vc_n = split_heads(cnv, NA_HEADS)
        y_na = na_latent(split_heads(nq, NA_HEADS), split_heads(nk, NA_HEADS), split_heads(nv, NA_HEADS), kc_n, vc_n, na_rpb[l])

        o = merge_branches(y_ml, y_ga, y_na, zg, w_branch[l], w_out[l])
        x = x + ga1 * rms_norm(o, g_post_mix[l])
        f = conv_ffn(modulate(rms_norm(x, g_pre_ffn[l]), sh2, sc2), w_up[l], conv_w[l], conv_b[l], w_down[l])
        x = x + ga2 * rms_norm(f, g_post_ffn[l])

        if not last:
            Tc = xc.shape[1]
            yc_ga = attend(qc_a.reshape(B, Tc, GQA_KV_HEADS, GQA_GROUP, HEAD_DIM), kc_a, vc_a).reshape(B, Tc, BRANCH_W)
            yc_na = attend(split_heads(cnq, NA_HEADS)[:, :, :, None, :], kc_n, vc_n).reshape(B, Tc, BRANCH_W)
            oc = merge_branches(yc_ml, yc_ga, yc_na, czg, w_branch[l], w_out[l])
            xc = xc + cga1 * rms_norm(oc, g_post_mix[l])
            fc = conv_ffn(modulate(rms_norm(xc, g_pre_ffn[l]), csh2, csc2), w_up[l], conv_w[l], conv_b[l], w_down[l])
            xc = xc + cga2 * rms_norm(fc, g_post_ffn[l])
    return x
```

```python
import functools

import numpy as np
import jax
import jax.numpy as jnp
from jax import lax
from jax.experimental import pallas as pl
from jax.experimental.pallas import tpu as pltpu

F32 = jnp.float32
BF16 = jnp.bfloat16

D_MODEL = 1024
BRANCH_W = 512
HEAD_DIM = 64
ML_HEADS = 4
ML_HEAD_DIM = 128
GRID_W = 64
NA_WIN_R = 8
NA_WIN_C = 16
ROPE_THETA = 10000.0
FFN_DIM = 2816
EPS = 1e-6
LANES = 128
ML_CHUNK = 256
NA_ROWS = 4
NEG = -1e30
VMEM_LIMIT = 56 * 1024 * 1024

NZ = 7424
COL_ZG = 0
COL_MQ, COL_MK, COL_MV, COL_MO = 3072, 3584, 4096, 4608
COL_AQ = 5120
COL_NQ, COL_NK, COL_NV = 5632, 6144, 6656
COL_AK, COL_AV = 7168, 7296

_NT = (((1,), (1,)), ((), ()))
_TN = (((0,), (0,)), ((), ()))


def _cparams(*sem):
    return pltpu.CompilerParams(dimension_semantics=sem, vmem_limit_bytes=VMEM_LIMIT)


def _rms(x):
    return x * lax.rsqrt(jnp.mean(x * x, axis=-1, keepdims=True) + EPS)


def _sigmoid(x):
    return 1.0 / (1.0 + jnp.exp(-x))


def _mod_body(c_ref, w_ref, b_ref, o_ref):
    c = c_ref[...]
    s = (c * _sigmoid(c)).astype(BF16)
    o_ref[0] = jnp.dot(s, w_ref[0].astype(BF16), preferred_element_type=F32) + b_ref[0]


def _mod_call(cc, w_mod, b_mod):
    depth, d, n = w_mod.shape
    tn = 1536
    return pl.pallas_call(
        _mod_body,
        grid=(depth, n // tn),
        in_specs=[pl.BlockSpec((cc.shape[0], d), lambda l, j: (0, 0)),
                  pl.BlockSpec((1, d, tn), lambda l, j: (l, 0, j)),
                  pl.BlockSpec((1, 1, tn), lambda l, j: (l, 0, j))],
        out_specs=pl.BlockSpec((1, cc.shape[0], tn), lambda l, j: (l, 0, j)),
        out_shape=jax.ShapeDtypeStruct((depth, cc.shape[0], n), F32),
        compiler_params=_cparams("parallel", "parallel"),
        name="modulation",
    )(cc, w_mod, b_mod.reshape(depth, 1, n))


def _inproj_body(x_ref, g_ref, mod_ref, w_ref, wg_ref, z_ref, zg_ref, h_scr):
    @pl.when(pl.program_id(2) == 0)
    def _():
        y = _rms(x_ref[0]) * g_ref[...]
        h = (y * (1.0 + mod_ref[0, 1:2, :]) + mod_ref[0, 0:1, :]).astype(BF16)
        h_scr[...] = h
        zg_ref[0] = jnp.dot(h, wg_ref[...], preferred_element_type=F32)

    z_ref[0] = jnp.dot(h_scr[...], w_ref[...], preferred_element_type=F32).astype(BF16)


def _mod_index(mod):
    if mod.shape[0] > 1:
        return lambda b, *_: (b, 0, 0)
    return lambda b, *_: (0, 0, 0)


def _inproj(x, g, mod, w, wg, tm):
    B, T, d = x.shape
    tn = NZ // 2
    return pl.pallas_call(
        _inproj_body,
        grid=(B, T // tm, NZ // tn),
        in_specs=[pl.BlockSpec((1, tm, d), lambda b, i, j: (b, i, 0)),
                  pl.BlockSpec((1, d), lambda b, i, j: (0, 0)),
                  pl.BlockSpec((1, 6, d), _mod_index(mod)),
                  pl.BlockSpec((d, tn), lambda b, i, j: (0, j)),
                  pl.BlockSpec((d, LANES), lambda b, i, j: (0, 0))],
        out_specs=[pl.BlockSpec((1, tm, tn), lambda b, i, j: (b, i, j)),
                   pl.BlockSpec((1, tm, LANES), lambda b, i, j: (b, i, 0))],
        out_shape=[jax.ShapeDtypeStruct((B, T, NZ), BF16), jax.ShapeDtypeStruct((B, T, LANES), F32)],
        scratch_shapes=[pltpu.VMEM((tm, d), BF16)],
        compiler_params=_cparams("parallel", "parallel", "arbitrary"),
        name="inproj",
    )(x, g.reshape(1, d), mod, w, wg)


def _log_sigmoid(x):
    return jnp.minimum(x, 0.0) - jnp.log1p(jnp.exp(-jnp.abs(x)))


def _split3(a):
    a1 = a.astype(BF16)
    r1 = a - a1.astype(F32)
    a2 = r1.astype(BF16)
    a3 = (r1 - a2.astype(F32)).astype(BF16)
    return a1, a2, a3


def _mlstm_chunk(q, k, v, b_col, b_row, ig_col, ig_row, b_last, mask, C, n, m):
    s0 = lax.dot_general(q, k, _NT, preferred_element_type=F32)
    logd = jnp.where(mask, b_col - b_row + ig_row, NEG)
    bm = b_col + m
    m_t = jnp.maximum(bm, jnp.max(logd, axis=-1, keepdims=True))
    s = s0 * jnp.exp(logd - m_t)
    inter = jnp.exp(bm - m_t)
    q_c = jnp.dot(q, C.astype(BF16), preferred_element_type=F32)
    num = jnp.dot(s.astype(BF16), v, preferred_element_type=F32) + inter * q_c
    q_n = jnp.sum(q.astype(F32) * n, axis=-1, keepdims=True)
    den = jnp.sum(s, axis=-1, keepdims=True) + inter * q_n
    h = num / jnp.maximum(jnp.abs(den), jnp.exp(-m_t))

    lw = b_last - b_col + ig_col
    m_new = jnp.maximum(b_last + m, jnp.max(lw, axis=0, keepdims=True))
    w = jnp.exp(lw - m_new)
    decay = jnp.exp(b_last + m - m_new)
    wv = (w * v.astype(F32)).astype(BF16)
    C_new = decay * C + lax.dot_general(k, wv, _TN, preferred_element_type=F32)
    n_new = decay * n + jnp.sum(w * k.astype(F32), axis=0, keepdims=True)
    return h, C_new, n_new, m_new


def _mlstm_body(qf_ref, kf_ref, vf_ref, gf_ref, qb_ref, kb_ref, vb_ref, gb_ref, bias_ref,
                c0f_ref, n0f_ref, m0f_ref, c0b_ref, n0b_ref, m0b_ref,
                hf_ref, hb_ref, c1f_ref, n1f_ref, m1f_ref, c1b_ref, n1b_ref, m1b_ref,
                cf_scr, nf_scr, mf_scr, cb_scr, nb_scr, mb_scr):
    i = pl.program_id(1)
    L = ML_CHUNK

    @pl.when(i == 0)
    def _():
        cf_scr[...] = c0f_ref[0]
        nf_scr[...] = n0f_ref[0]
        mf_scr[...] = m0f_ref[0]
        cb_scr[...] = c0b_ref[0]
        nb_scr[...] = n0b_ref[0]
        mb_scr[...] = m0b_ref[0]

    row = lax.broadcasted_iota(jnp.int32, (L, L), 0)
    col = lax.broadcasted_iota(jnp.int32, (L, L), 1)
    lane = lax.broadcasted_iota(jnp.int32, (L, LANES), 1)
    is_forget = (lane % 8) >= 4

    def run(fwd, q_ref, k_ref, v_ref, g_ref, h_ref, c_scr, n_scr, m_scr):
        mask = (row >= col) if fwd else (row <= col)
        tri = jnp.where(mask, 1.0, 0.0).astype(BF16)
        g = g_ref[0] + bias_ref[...]
        G = jnp.where(is_forget, _log_sigmoid(g), g)
        Bc = sum(jnp.dot(tri, part, preferred_element_type=F32) for part in _split3(G))
        GT = G.T
        BT = Bc.T
        base = 0 if fwd else 2 * ML_HEADS
        for hh in range(ML_HEADS):
            ci, cf = base + hh, base + ML_HEADS + hh
            b_col = Bc[:, cf:cf + 1]
            b_last = b_col[L - 1:L, :] if fwd else b_col[0:1, :]
            sl = slice(hh * ML_HEAD_DIM, (hh + 1) * ML_HEAD_DIM)
            h, C, n, m = _mlstm_chunk(
                q_ref[0, :, sl], k_ref[0, :, sl], v_ref[0, :, sl],
                b_col, BT[cf:cf + 1, :], G[:, ci:ci + 1], GT[ci:ci + 1, :], b_last, mask,
                c_scr[hh], n_scr[hh], m_scr[hh][:, 0:1])
            h_ref[0, :, sl] = h
            c_scr[hh] = C
            n_scr[hh] = n
            m_scr[hh] = jnp.broadcast_to(m, (1, LANES))

    run(True, qf_ref, kf_ref, vf_ref, gf_ref, hf_ref, cf_scr, nf_scr, mf_scr)
    run(False, qb_ref, kb_ref, vb_ref, gb_ref, hb_ref, cb_scr, nb_scr, mb_scr)

    @pl.when(i == pl.num_programs(1) - 1)
    def _():
        c1f_ref[0] = cf_scr[...]
        n1f_ref[0] = nf_scr[...]
        m1f_ref[0] = mf_scr[...]
        c1b_ref[0] = cb_scr[...]
        n1b_ref[0] = nb_scr[...]
        m1b_ref[0] = mb_scr[...]


def _mlstm(z, zg, bias, state):
    B, T, _ = z.shape
    L = ML_CHUNK
    nc = T // L
    H, dh = ML_HEADS, ML_HEAD_DIM
    bq, bk, bv = COL_MQ // BRANCH_W, COL_MK // BRANCH_W, COL_MV // BRANCH_W
    fw = lambda blk: (lambda b, i: (b, i, blk))
    bw = lambda blk: (lambda b, i: (b, nc - 1 - i, blk))
    st4 = lambda b, i: (b, 0, 0, 0)
    c_spec = pl.BlockSpec((1, H, dh, dh), st4)
    v_spec = pl.BlockSpec((1, H, 1, LANES), st4)
    c_shape = jax.ShapeDtypeStruct((B, H, dh, dh), F32)
    v_shape = jax.ShapeDtypeStruct((B, H, 1, LANES), F32)
    tok = lambda idx: pl.BlockSpec((1, L, BRANCH_W), idx)
    gsp = lambda idx: pl.BlockSpec((1, L, LANES), idx)
    outs = pl.pallas_call(
        _mlstm_body,
        grid=(B, nc),
        in_specs=[tok(fw(bq)), tok(fw(bk)), tok(fw(bv)), gsp(fw(0)),
                  tok(bw(bq)), tok(bw(bk)), tok(bw(bv)), gsp(bw(0)),
                  pl.BlockSpec((1, LANES), lambda b, i: (0, 0)),
                  c_spec, v_spec, v_spec, c_spec, v_spec, v_spec],
        out_specs=[tok(fw(0)), tok(bw(0)), c_spec, v_spec, v_spec, c_spec, v_spec, v_spec],
        out_shape=[jax.ShapeDtypeStruct((B, T, BRANCH_W), F32)] * 2 + [c_shape, v_shape, v_shape] * 2,
        scratch_shapes=[pltpu.VMEM((H, dh, dh), F32), pltpu.VMEM((H, 1, LANES), F32), pltpu.VMEM((H, 1, LANES), F32)] * 2,
        compiler_params=_cparams("parallel", "arbitrary"),
        name="mlstm_scan",
    )(z, z, z, zg, z, z, z, zg, bias, *state)
    return outs[0], outs[1], tuple(outs[2:])


def _qkprep_body(rope, q_ref, k_ref, gq_ref, gk_ref, *rest):
    if rope:
        cos_ref, sa_ref, sb_ref, qo_ref, ko_ref = rest
    else:
        qo_ref, ko_ref = rest
    tm = q_ref.shape[1]
    r = lax.broadcasted_iota(jnp.int32, (LANES, LANES), 0) // HEAD_DIM
    c = lax.broadcasted_iota(jnp.int32, (LANES, LANES), 1) // HEAD_DIM
    head_ones = jnp.where(r == c, 1.0, 0.0).astype(BF16)
    lo = lax.broadcasted_iota(jnp.int32, (tm, LANES), 1) < HEAD_DIM

    def norm_rope(x, g):
        ss = jnp.dot((x * x).astype(BF16), head_ones, preferred_element_type=F32)
        y = x * lax.rsqrt(ss * (1.0 / HEAD_DIM) + EPS) * g
        if rope:
            y = y * cos_ref[...] + pltpu.roll(y, LANES - 16, 1) * sa_ref[...] + pltpu.roll(y, 16, 1) * sb_ref[...]
        return y

    for j in range(BRANCH_W // LANES):
        y = norm_rope(q_ref[0, :, j * LANES:(j + 1) * LANES].astype(F32), gq_ref[...]) * (HEAD_DIM ** -0.5)
        even = jnp.where(lo, y, 0.0)
        odd = jnp.where(lo, 0.0, y)
        if j // 2 == 0:
            odd = pltpu.roll(odd, HEAD_DIM, 1)
        else:
            even = pltpu.roll(even, HEAD_DIM, 1)
        qo_ref[0, :, (2 * j) * LANES:(2 * j + 1) * LANES] = even.astype(BF16)
        qo_ref[0, :, (2 * j + 1) * LANES:(2 * j + 2) * LANES] = odd.astype(BF16)
    ko_ref[0] = norm_rope(k_ref[0].astype(F32), gk_ref[...]).astype(BF16)


def _qkprep(z, gq, gk, tables, tm):
    B, T, _ = z.shape
    rope = tables is not None
    in_specs = [pl.BlockSpec((1, tm, BRANCH_W), lambda b, i: (b, i, COL_AQ // BRANCH_W)),
                pl.BlockSpec((1, tm, LANES), lambda b, i: (b, i, COL_AK // LANES)),
                pl.BlockSpec((1, LANES), lambda b, i: (0, 0)),
                pl.BlockSpec((1, LANES), lambda b, i: (0, 0))]
    args = [z, z, gq, gk]
    if rope:
        in_specs += [pl.BlockSpec((tm, LANES), lambda b, i: (i, 0))] * 3
        args += list(tables)
    return pl.pallas_call(
        functools.partial(_qkprep_body, rope),
        grid=(B, T // tm),
        in_specs=in_specs,
        out_specs=[pl.BlockSpec((1, tm, 2 * BRANCH_W), lambda b, i: (b, i, 0)),
                   pl.BlockSpec((1, tm, LANES), lambda b, i: (b, i, 0))],
        out_shape=[jax.ShapeDtypeStruct((B, T, 2 * BRANCH_W), BF16), jax.ShapeDtypeStruct((B, T, LANES), BF16)],
        compiler_params=_cparams("parallel", "parallel"),
        name="qk_prep",
    )(*args)


def _rope_tables(n_tok):
    half = HEAD_DIM // 2
    t = jnp.arange(n_tok)
    inv = ROPE_THETA ** (-jnp.arange(0, half, 2, dtype=F32) / half)
    ang_r = (t // GRID_W).astype(F32)[:, None] * inv
    ang_c = (t % GRID_W).astype(F32)[:, None] * inv
    ang = jnp.concatenate([ang_r, ang_r, ang_c, ang_c], axis=-1)
    cos, sin = jnp.cos(ang), jnp.sin(ang)
    even_quarter = ((np.arange(HEAD_DIM) // (HEAD_DIM // 4)) % 2 == 0)[None, :]
    sa = jnp.where(even_quarter, -sin, 0.0)
    sb = jnp.where(even_quarter, 0.0, sin)
    tile = lambda a: jnp.concatenate([a, a], axis=-1)
    return tile(cos), tile(sa), tile(sb)


def _gqa_body(tk, q_ref, k_ref, v_ref, o_ref, m_scr, l_scr, acc_scr):
    g = pl.program_id(1)
    tq = q_ref.shape[1]
    S = k_ref.shape[1]
    n_heads = BRANCH_W // LANES
    q = jnp.concatenate([q_ref[0, :, h * LANES:(h + 1) * LANES] for h in range(n_heads)], axis=0)
    m_scr[...] = jnp.full(m_scr.shape, -jnp.inf, F32)
    l_scr[...] = jnp.zeros(l_scr.shape, F32)
    acc_scr[...] = jnp.zeros(acc_scr.shape, F32)

    def step(c, carry):
        off = pl.multiple_of(c * tk, tk)
        k = k_ref[0, pl.ds(off, tk), :]
        v = v_ref[0, pl.ds(off, tk), :]
        s = lax.dot_general(q, k, _NT, preferred_element_type=F32)
        m_prev = m_scr[...]
        m_new = jnp.maximum(m_prev, jnp.max(s, axis=-1, keepdims=True))
        alpha = jnp.exp(m_prev - m_new)
        p = jnp.exp(s - m_new)
        l_scr[...] = alpha * l_scr[...] + jnp.sum(p, axis=-1, keepdims=True)
        acc_scr[...] = alpha * acc_scr[...] + jnp.dot(p.astype(BF16), v, preferred_element_type=F32)
        m_scr[...] = m_new
        return carry

    lax.fori_loop(0, S // tk, step, 0)
    o = acc_scr[...] / l_scr[...]
    lo = lax.broadcasted_iota(jnp.int32, (tq, LANES), 1) < HEAD_DIM
    first = g == 0
    for j in range(n_heads // 2):
        a = o[(2 * j) * tq:(2 * j + 1) * tq]
        b = o[(2 * j + 1) * tq:(2 * j + 2) * tq]
        pair0 = jnp.where(lo, a, pltpu.roll(b, HEAD_DIM, 1))
        pair1 = jnp.where(lo, pltpu.roll(a, HEAD_DIM, 1), b)
        o_ref[0, :, j * LANES:(j + 1) * LANES] = jnp.where(first, pair0, pair1).astype(BF16)


def _gqa(qp, k_all, v_all, tq, tk):
    B, T, _ = qp.shape
    S = k_all.shape[1]
    n_kv = 2
    return pl.pallas_call(
        functools.partial(_gqa_body, tk),
        grid=(B, n_kv, T // tq),
        in_specs=[pl.BlockSpec((1, tq, BRANCH_W), lambda b, g, i: (b, i, g)),
                  pl.BlockSpec((1, S, LANES), lambda b, g, i: (b, 0, 0)),
                  pl.BlockSpec((1, S, LANES), lambda b, g, i: (b, 0, 0))],
        out_specs=pl.BlockSpec((1, tq, BRANCH_W // n_kv), lambda b, g, i: (b, i, g)),
        out_shape=jax.ShapeDtypeStruct((B, T, BRANCH_W), BF16),
        scratch_shapes=[pltpu.VMEM((4 * tq, 1), F32), pltpu.VMEM((4 * tq, 1), F32), pltpu.VMEM((4 * tq, LANES), F32)],
        compiler_params=_cparams("parallel", "parallel", "parallel"),
        name="gqa_flash",
    )(qp, k_all, v_all)


def _na_body(has_win, wk, rows, q_ref, *rest):
    if has_win:
        k_ref, v_ref, kc_ref, vc_ref, bm_ref, o_ref = rest
    else:
        kc_ref, vc_ref, o_ref = rest
    i = pl.program_id(1)
    tq = q_ref.shape[1]
    lo = lax.broadcasted_iota(jnp.int32, (tq, LANES), 1) < HEAD_DIM
    if has_win:
        ks = jnp.clip(i * NA_ROWS - NA_WIN_R // 2, 0, rows - wk // GRID_W)
        off = pl.multiple_of(ks * GRID_W, GRID_W)
    for j in range(BRANCH_W // LANES):
        sl = slice(j * LANES, (j + 1) * LANES)
        q2 = q_ref[0, :, sl] * (HEAD_DIM ** -0.5)
        kc = kc_ref[0, :, sl]
        vc = vc_ref[0, :, sl]
        if has_win:
            kw = k_ref[0, pl.ds(off, wk), sl]
            vw = v_ref[0, pl.ds(off, wk), sl]
        halves = []
        for e in range(2):
            qh = jnp.where(lo if e == 0 else jnp.logical_not(lo), q2, jnp.zeros_like(q2))
            sc = lax.dot_general(qh, kc, _NT, preferred_element_type=F32)
            m = jnp.max(sc, axis=-1, keepdims=True)
            if has_win:
                sw = lax.dot_general(qh, kw, _NT, preferred_element_type=F32) + bm_ref[0, 2 * j + e]
                m = jnp.maximum(m, jnp.max(sw, axis=-1, keepdims=True))
            pc = jnp.exp(sc - m)
            l = jnp.sum(pc, axis=-1, keepdims=True)
            acc = jnp.dot(pc.astype(BF16), vc, preferred_element_type=F32)
            if has_win:
                pw = jnp.exp(sw - m)
                l = l + jnp.sum(pw, axis=-1, keepdims=True)
                acc = acc + jnp.dot(pw.astype(BF16), vw, preferred_element_type=F32)
            halves.append(acc / l)
        o_ref[0, :, sl] = jnp.where(lo, halves[0], halves[1]).astype(BF16)


def _na(z, zc, bm):
    B, T, _ = z.shape
    Tc = zc.shape[1]
    rows = T // GRID_W
    tq = NA_ROWS * GRID_W
    wk = (NA_ROWS + NA_WIN_R - 1) * GRID_W
    nb = rows // NA_ROWS
    bq, bk, bv = COL_NQ // BRANCH_W, COL_NK // BRANCH_W, COL_NV // BRANCH_W

    def bm_index(b, i):
        return (jnp.where(i == 0, 0, jnp.where(i == nb - 1, 2, 1)), 0, 0, 0)

    return pl.pallas_call(
        functools.partial(_na_body, True, wk, rows),
        grid=(B, nb),
        in_specs=[pl.BlockSpec((1, tq, BRANCH_W), lambda b, i: (b, i, bq)),
                  pl.BlockSpec((1, T, BRANCH_W), lambda b, i: (b, 0, bk)),
                  pl.BlockSpec((1, T, BRANCH_W), lambda b, i: (b, 0, bv)),
                  pl.BlockSpec((1, Tc, BRANCH_W), lambda b, i: (b, 0, bk)),
                  pl.BlockSpec((1, Tc, BRANCH_W), lambda b, i: (b, 0, bv)),
                  pl.BlockSpec((1, 2 * BRANCH_W // LANES, tq, wk), bm_index)],
        out_specs=pl.BlockSpec((1, tq, BRANCH_W), lambda b, i: (b, i, 0)),
        out_shape=jax.ShapeDtypeStruct((B, T, BRANCH_W), BF16),
        compiler_params=_cparams("parallel", "arbitrary"),
        name="na_window",
    )(z, z, z, zc, zc, bm)


def _na_ctx(zc):
    B, Tc, _ = zc.shape
    bq, bk, bv = COL_NQ // BRANCH_W, COL_NK // BRANCH_W, COL_NV // BRANCH_W
    return pl.pallas_call(
        functools.partial(_na_body, False, 0, 0),
        grid=(B, 1),
        in_specs=[pl.BlockSpec((1, Tc, BRANCH_W), lambda b, i: (b, 0, bq)),
                  pl.BlockSpec((1, Tc, BRANCH_W), lambda b, i: (b, 0, bk)),
                  pl.BlockSpec((1, Tc, BRANCH_W), lambda b, i: (b, 0, bv))],
        out_specs=pl.BlockSpec((1, Tc, BRANCH_W), lambda b, i: (b, 0, 0)),
        out_shape=jax.ShapeDtypeStruct((B, Tc, BRANCH_W), BF16),
        compiler_params=_cparams("parallel", "arbitrary"),
        name="na_context",
    )(zc, zc, zc)


def _na_bias_tables(rpb, rows):
    R, W = NA_ROWS, GRID_W
    wr, wc = min(NA_WIN_R, rows), NA_WIN_C
    nkr = R + NA_WIN_R - 1
    nb = rows // R
    dr_all, dc_all, ok_all = [], [], []
    for blk in (0, 1, nb - 1):
        rb = blk * R
        ks = int(np.clip(rb - NA_WIN_R // 2, 0, rows - nkr))
        r = rb + np.arange(R)[:, None, None, None]
        c = np.arange(W)[None, :, None, None]
        a = ks + np.arange(nkr)[None, None, :, None]
        kc = np.arange(W)[None, None, None, :]
        r0 = np.clip(r - wr // 2, 0, rows - wr)
        c0 = np.clip(c - wc // 2, 0, W - wc)
        ok = (a >= r0) & (a < r0 + wr) & (kc >= c0) & (kc < c0 + wc)
        dr = np.clip(a - r + (NA_WIN_R - 1), 0, 2 * NA_WIN_R - 2)
        dc = np.clip(kc - c + (NA_WIN_C - 1), 0, 2 * NA_WIN_C - 2)
        shp = (R * W, nkr * W)
        dr_all.append(np.broadcast_to(dr, ok.shape).reshape(shp))
        dc_all.append(np.broadcast_to(dc, ok.shape).reshape(shp))
        ok_all.append(ok.reshape(shp))
    dr, dc, ok = (np.stack(t) for t in (dr_all, dc_all, ok_all))
    bias = rpb[:, :, dr, dc]
    return jnp.where(ok[None, None], bias, NEG).transpose(0, 2, 1, 3, 4).astype(F32)


def _merge_body(hf_ref, hb_ref, zo_ref, zg_ref, ga_ref, na_ref, x_ref, gml_ref, wb_ref, wo_ref, gpost_ref, mod_ref, o_ref):
    h = hf_ref[0] + hb_ref[0]
    hn = jnp.concatenate([_rms(h[:, t * ML_HEAD_DIM:(t + 1) * ML_HEAD_DIM]) for t in range(ML_HEADS)], axis=-1)
    y_ml = (hn * gml_ref[...] * _sigmoid(zo_ref[0].astype(F32))).astype(BF16)
    acc = None
    for n, y in enumerate((y_ml, ga_ref[0], na_ref[0])):
        gate = _sigmoid(zg_ref[0, :, n * D_MODEL:(n + 1) * D_MODEL].astype(F32))
        term = gate * jnp.dot(y, wb_ref[n], preferred_element_type=F32)
        acc = term if acc is None else acc + term
    o = jnp.dot(acc.astype(BF16), wo_ref[...], preferred_element_type=F32)
    o_ref[0] = x_ref[0] + mod_ref[0, 2:3, :] * (_rms(o) * gpost_ref[...])


def _merge(hf, hb, z, y_ga, y_na, x, g_ml, wb, wo, g_post, mod, tm):
    B, T, d = x.shape
    tok = lambda w, blk: pl.BlockSpec((1, tm, w), lambda b, i: (b, i, blk))
    full = lambda a: pl.BlockSpec(a.shape, lambda b, i: (0,) * a.ndim)
    g_ml = g_ml.reshape(1, BRANCH_W)
    g_post = g_post.reshape(1, d)
    return pl.pallas_call(
        _merge_body,
        grid=(B, T // tm),
        in_specs=[tok(BRANCH_W, 0), tok(BRANCH_W, 0), tok(BRANCH_W, COL_MO // BRANCH_W), tok(3 * d, 0),
                  tok(BRANCH_W, 0), tok(BRANCH_W, 0), tok(d, 0),
                  full(g_ml), full(wb), full(wo), full(g_post),
                  pl.BlockSpec((1, 6, d), _mod_index(mod))],
        out_specs=tok(d, 0),
        out_shape=jax.ShapeDtypeStruct((B, T, d), F32),
        compiler_params=_cparams("parallel", "parallel"),
        name="merge_out",
    )(hf, hb, z, z, y_ga, y_na, x, g_ml, wb, wo, g_post, mod)


def _ffn_body(x_ref, xp_ref, xn_ref, g_ref, mod_ref, wa_ref, wg_ref, cwa_ref, cwg_ref, cba_ref, cbg_ref, wd_ref, gpost_ref,
              o_ref, h_scr, halo_scr, acc_scr):
    i = pl.program_id(1)
    f = pl.program_id(2)
    tm = x_ref.shape[1]
    edge = xp_ref.shape[1]

    @pl.when(f == 0)
    def _():
        def nm(x):
            return (_rms(x) * g_ref[...] * (1.0 + mod_ref[0, 4:5, :]) + mod_ref[0, 3:4, :]).astype(BF16)
        h_scr[...] = nm(x_ref[0])
        halo_scr[0:edge] = nm(xp_ref[0])
        halo_scr[edge:2 * edge] = nm(xn_ref[0])
        acc_scr[...] = jnp.zeros(acc_scr.shape, F32)

    h2 = h_scr[...]
    halo = halo_scr[...]
    has_prev = i > 0
    has_next = i < pl.num_programs(1) - 1

    def conv_branch(w_ref, cw_ref, cb_ref):
        w = w_ref[...]
        u = jnp.dot(h2, w, preferred_element_type=F32)
        uh = jnp.dot(halo, w, preferred_element_type=F32)
        prev_row = jnp.where(has_prev, uh[edge - 1:edge], 0.0)
        next_row = jnp.where(has_next, uh[edge:edge + 1], 0.0)
        r = lax.broadcasted_iota(jnp.int32, u.shape, 0)
        u_prev = jnp.where(r == 0, prev_row, pltpu.roll(u, 1, 0))
        u_next = jnp.where(r == tm - 1, next_row, pltpu.roll(u, tm - 1, 0))
        return u_prev * cw_ref[0:1, :] + u * cw_ref[1:2, :] + u_next * cw_ref[2:3, :] + cb_ref[...]

    a = conv_branch(wa_ref, cwa_ref, cba_ref)
    g = conv_branch(wg_ref, cwg_ref, cbg_ref)
    act = (a * (g * _sigmoid(g))).astype(BF16)
    acc_scr[...] += jnp.dot(act, wd_ref[...], preferred_element_type=F32)

    @pl.when(f == pl.num_programs(2) - 1)
    def _():
        o_ref[0] = x_ref[0] + mod_ref[0, 5:6, :] * (_rms(acc_scr[...]) * gpost_ref[...])


def _ffn(x, g_pre, mod, w_up, cw, cb, w_down, g_post, tm, tf):
    B, T, d = x.shape
    edge = 8
    nf = FFN_DIM // tf
    per = tm // edge
    last_edge = T // edge - 1
    return pl.pallas_call(
        _ffn_body,
        grid=(B, T // tm, nf),
        in_specs=[pl.BlockSpec((1, tm, d), lambda b, i, f: (b, i, 0)),
                  pl.BlockSpec((1, edge, d), lambda b, i, f: (b, jnp.maximum(i * per - 1, 0), 0)),
                  pl.BlockSpec((1, edge, d), lambda b, i, f: (b, jnp.minimum((i + 1) * per, last_edge), 0)),
                  pl.BlockSpec((1, d), lambda b, i, f: (0, 0)),
                  pl.BlockSpec((1, 6, d), _mod_index(mod)),
                  pl.BlockSpec((d, tf), lambda b, i, f: (0, f)),
                  pl.BlockSpec((d, tf), lambda b, i, f: (0, nf + f)),
                  pl.BlockSpec((3, tf), lambda b, i, f: (0, f)),
                  pl.BlockSpec((3, tf), lambda b, i, f: (0, nf + f)),
                  pl.BlockSpec((1, tf), lambda b, i, f: (0, f)),
                  pl.BlockSpec((1, tf), lambda b, i, f: (0, nf + f)),
                  pl.BlockSpec((tf, d), lambda b, i, f: (f, 0)),
                  pl.BlockSpec((1, d), lambda b, i, f: (0, 0))],
        out_specs=pl.BlockSpec((1, tm, d), lambda b, i, f: (b, i, 0)),
        out_shape=jax.ShapeDtypeStruct((B, T, d), F32),
        scratch_shapes=[pltpu.VMEM((tm, d), BF16), pltpu.VMEM((2 * edge, d), BF16), pltpu.VMEM((tm, d), F32)],
        compiler_params=_cparams("parallel", "parallel", "arbitrary"),
        name="conv_ffn",
    )(x, x, x, g_pre.reshape(1, d), mod, w_up, w_up, cw, cw, cb.reshape(1, -1), cb.reshape(1, -1), w_down, g_post.reshape(1, d))


def _prep_w_in(w_in):
    sizes = (512, 512, 512, 512, 16, 512, 128, 128, 512, 512, 512, 3072)
    offs = np.concatenate([[0], np.cumsum(sizes)])
    seg = lambda n: w_in[:, :, offs[n]:offs[n + 1]]
    mq, mk, mv, mo, mg, aq, ak, av, nq, nk, nv, zg = (seg(n) for n in range(12))
    mk = mk * (ML_HEAD_DIM ** -0.5)
    main = jnp.concatenate([zg, mq, mk, mv, mo, aq, nq, nk, nv, ak, av], axis=-1).astype(BF16)
    gate = jnp.pad(mg, ((0, 0), (0, 0), (0, LANES - mg.shape[-1]))).astype(BF16)
    return main, gate


def kernel(x, c, ctx, c_ctx, w_mod, b_mod, g_pre_mix, g_post_mix, g_pre_ffn, g_post_ffn, w_in, b_ml_gates, g_ml_out, g_q, g_k, na_rpb, w_branch, w_out, w_up, conv_w, conv_b, w_down):
    B, T, d = x.shape
    Tc = ctx.shape[1]
    depth = w_mod.shape[0]
    rows = T // GRID_W

    w_main, w_gate = _prep_w_in(w_in)
    wb, wo, wu, wd = (a.astype(BF16) for a in (w_branch, w_out, w_up, w_down))
    n_mod = -(-(B + 1) // 8) * 8
    cc = jnp.concatenate([c, c_ctx[None], jnp.zeros((n_mod - B - 1, d), F32)], axis=0)
    mod_all = _mod_call(cc, w_mod, b_mod)
    tables = _rope_tables(T)
    bias_tables = _na_bias_tables(na_rpb, rows)
    gate_bias = jnp.pad(b_ml_gates, ((0, 0), (0, LANES - b_ml_gates.shape[-1])))
    gq2 = jnp.concatenate([g_q, g_q], axis=-1)
    gk2 = jnp.concatenate([g_k, g_k], axis=-1)
    H, dh = ML_HEADS, ML_HEAD_DIM
    zero_state = (jnp.zeros((B, H, dh, dh), F32), jnp.zeros((B, H, 1, LANES), F32), jnp.zeros((B, H, 1, LANES), F32)) * 2

    tm = min(512, T)
    tq = min(128, T)
    xc = ctx
    for l in range(depth):
        last = l == depth - 1
        mod_l = mod_all[l, :B].reshape(B, 6, d)
        mod_c = mod_all[l, B:B + 1].reshape(1, 6, d)
        bias_l = gate_bias[l][None]

        z, zg = _inproj(x, g_pre_mix[l], mod_l, w_main[l], w_gate[l], tm)
        zc, zgc = _inproj(xc, g_pre_mix[l], mod_c, w_main[l], w_gate[l], Tc)

        hfc, hbc, state = _mlstm(zc, zgc, bias_l, zero_state)
        hf, hb, _ = _mlstm(z, zg, bias_l, state)

        qp, kn = _qkprep(z, gq2[l][None], gk2[l][None], tables, tm)
        qpc, knc = _qkprep(zc, gq2[l][None], gk2[l][None], None, Tc)
        vc_a = zc[:, :, COL_AV:COL_AV + LANES]
        k_all = jnp.concatenate([knc, kn], axis=1)
        v_all = jnp.concatenate([vc_a, z[:, :, COL_AV:COL_AV + LANES]], axis=1)
        y_ga = _gqa(qp, k_all, v_all, tq, ML_CHUNK)
        y_na = _na(z, zc, bias_tables[l])

        x_mid = _merge(hf, hb, z, y_ga, y_na, x, g_ml_out[l], wb[l], wo[l], g_post_mix[l], mod_l, tm)
        x_new = _ffn(x_mid, g_pre_ffn[l], mod_l, wu[l], conv_w[l], conv_b[l], wd[l], g_post_ffn[l], tm, FFN_DIM // 2)

        if not last:
            yc_ga = _gqa(qpc, knc, vc_a, min(128, Tc), Tc)
            yc_na = _na_ctx(zc)
            xc_mid = _merge(hfc, hbc, zc, yc_ga, yc_na, xc, g_ml_out[l], wb[l], wo[l], g_post_mix[l], mod_c, Tc)
            xc = _ffn(xc_mid, g_pre_ffn[l], mod_c, wu[l], conv_w[l], conv_b[l], wd[l], g_post_ffn[l], Tc, FFN_DIM // 2)
        x = x_new
    return x
```

```python
import functools

import numpy as np
import jax
import jax.numpy as jnp
from jax import lax
from jax.experimental import pallas as pl
from jax.experimental.pallas import tpu as pltpu

F32 = jnp.float32
BF16 = jnp.bfloat16

D_MODEL = 1024
BRANCH_W = 512
HEAD_DIM = 64
ML_HEADS = 4
ML_HEAD_DIM = 128
GRID_W = 64
NA_WIN_R = 8
NA_WIN_C = 16
ROPE_THETA = 10000.0
FFN_DIM = 2816
EPS = 1e-6
LANES = 128
ML_CHUNK = 256
NA_ROWS = 4
GQA_KEY_CHUNK = 1024
NEG = -1e30
VMEM_LIMIT = 56 * 1024 * 1024

NZ = 7424
COL_ZG = 0
COL_MQ, COL_MK, COL_MV, COL_MO = 3072, 3584, 4096, 4608
COL_AQ = 5120
COL_NQ, COL_NK, COL_NV = 5632, 6144, 6656
COL_AK, COL_AV = 7168, 7296

_NT = (((1,), (1,)), ((), ()))
_TN = (((0,), (0,)), ((), ()))


def _cparams(*sem):
    return pltpu.CompilerParams(dimension_semantics=sem, vmem_limit_bytes=VMEM_LIMIT)


def _rms(x):
    return x * lax.rsqrt(jnp.mean(x * x, axis=-1, keepdims=True) + EPS)


def _sigmoid(x):
    return 1.0 / (1.0 + jnp.exp(-x))


def _mod_body(c_ref, w_ref, b_ref, o_ref):
    c = c_ref[...]
    s = (c * _sigmoid(c)).astype(BF16)
    o_ref[0] = jnp.dot(s, w_ref[0].astype(BF16), preferred_element_type=F32) + b_ref[0]


def _mod_call(cc, w_mod, b_mod):
    depth, d, n = w_mod.shape
    tn = 1536
    return pl.pallas_call(
        _mod_body,
        grid=(depth, n // tn),
        in_specs=[pl.BlockSpec((cc.shape[0], d), lambda l, j: (0, 0)),
                  pl.BlockSpec((1, d, tn), lambda l, j: (l, 0, j)),
                  pl.BlockSpec((1, 1, tn), lambda l, j: (l, 0, j))],
        out_specs=pl.BlockSpec((1, cc.shape[0], tn), lambda l, j: (l, 0, j)),
        out_shape=jax.ShapeDtypeStruct((depth, cc.shape[0], n), F32),
        compiler_params=_cparams("parallel", "parallel"),
        name="modulation",
    )(cc, w_mod, b_mod.reshape(depth, 1, n))


def _inproj_body(x_ref, g_ref, mod_ref, w_ref, wg_ref, z_ref, zg_ref, h_scr):
    @pl.when(pl.program_id(2) == 0)
    def _():
        y = _rms(x_ref[0]) * g_ref[...]
        h = (y * (1.0 + mod_ref[0, 1:2, :]) + mod_ref[0, 0:1, :]).astype(BF16)
        h_scr[...] = h
        zg_ref[0] = jnp.dot(h, wg_ref[...], preferred_element_type=F32)

    z_ref[0] = jnp.dot(h_scr[...], w_ref[...], preferred_element_type=F32).astype(BF16)


def _mod_index(mod):
    if mod.shape[0] > 1:
        return lambda b, *_: (b, 0, 0)
    return lambda b, *_: (0, 0, 0)


def _inproj(x, g, mod, w, wg, tm):
    B, T, d = x.shape
    tn = NZ // 2
    return pl.pallas_call(
        _inproj_body,
        grid=(B, T // tm, NZ // tn),
        in_specs=[pl.BlockSpec((1, tm, d), lambda b, i, j: (b, i, 0)),
                  pl.BlockSpec((1, d), lambda b, i, j: (0, 0)),
                  pl.BlockSpec((1, 6, d), _mod_index(mod)),
                  pl.BlockSpec((d, tn), lambda b, i, j: (0, j)),
                  pl.BlockSpec((d, LANES), lambda b, i, j: (0, 0))],
        out_specs=[pl.BlockSpec((1, tm, tn), lambda b, i, j: (b, i, j)),
                   pl.BlockSpec((1, tm, LANES), lambda b, i, j: (b, i, 0))],
        out_shape=[jax.ShapeDtypeStruct((B, T, NZ), BF16), jax.ShapeDtypeStruct((B, T, LANES), F32)],
        scratch_shapes=[pltpu.VMEM((tm, d), BF16)],
        compiler_params=_cparams("parallel", "parallel", "arbitrary"),
        name="inproj",
    )(x, g.reshape(1, d), mod, w, wg)


def _log_sigmoid(x):
    return jnp.minimum(x, 0.0) - jnp.log1p(jnp.exp(-jnp.abs(x)))


def _split3(a):
    a1 = a.astype(BF16)
    r1 = a - a1.astype(F32)
    a2 = r1.astype(BF16)
    a3 = (r1 - a2.astype(F32)).astype(BF16)
    return a1, a2, a3


def _mlstm_chunk(q, k, v, b_col, b_row, ig_col, ig_row, b_last, mask, C, n, m):
    s0 = lax.dot_general(q, k, _NT, preferred_element_type=F32)
    logd = jnp.where(mask, b_col - b_row + ig_row, NEG)
    bm = b_col + m
    m_t = jnp.maximum(bm, jnp.max(logd, axis=-1, keepdims=True))
    s = s0 * jnp.exp(logd - m_t)
    inter = jnp.exp(bm - m_t)
    q_c = jnp.dot(q, C.astype(BF16), preferred_element_type=F32)
    num = jnp.dot(s.astype(BF16), v, preferred_element_type=F32) + inter * q_c
    q_n = jnp.sum(q.astype(F32) * n, axis=-1, keepdims=True)
    den = jnp.sum(s, axis=-1, keepdims=True) + inter * q_n
    h = num / jnp.maximum(jnp.abs(den), jnp.exp(-m_t))

    lw = b_last - b_col + ig_col
    m_new = jnp.maximum(b_last + m, jnp.max(lw, axis=0, keepdims=True))
    w = jnp.exp(lw - m_new)
    decay = jnp.exp(b_last + m - m_new)
    wv = (w * v.astype(F32)).astype(BF16)
    C_new = decay * C + lax.dot_general(k, wv, _TN, preferred_element_type=F32)
    n_new = decay * n + jnp.sum(w * k.astype(F32), axis=0, keepdims=True)
    return h, C_new, n_new, m_new


def _mlstm_body(qf_ref, kf_ref, vf_ref, gf_ref, qb_ref, kb_ref, vb_ref, gb_ref, bias_ref,
                c0f_ref, n0f_ref, m0f_ref, c0b_ref, n0b_ref, m0b_ref,
                hf_ref, hb_ref, c1f_ref, n1f_ref, m1f_ref, c1b_ref, n1b_ref, m1b_ref,
                cf_scr, nf_scr, mf_scr, cb_scr, nb_scr, mb_scr):
    i = pl.program_id(1)
    L = ML_CHUNK

    @pl.when(i == 0)
    def _():
        cf_scr[...] = c0f_ref[0]
        nf_scr[...] = n0f_ref[0]
        mf_scr[...] = m0f_ref[0]
        cb_scr[...] = c0b_ref[0]
        nb_scr[...] = n0b_ref[0]
        mb_scr[...] = m0b_ref[0]

    row = lax.broadcasted_iota(jnp.int32, (L, L), 0)
    col = lax.broadcasted_iota(jnp.int32, (L, L), 1)
    lane = lax.broadcasted_iota(jnp.int32, (L, LANES), 1)
    is_forget = (lane % 8) >= 4

    def run(fwd, q_ref, k_ref, v_ref, g_ref, h_ref, c_scr, n_scr, m_scr):
        mask = (row >= col) if fwd else (row <= col)
        tri = jnp.where(mask, 1.0, 0.0).astype(BF16)
        g = g_ref[0] + bias_ref[...]
        G = jnp.where(is_forget, _log_sigmoid(g), g)
        Bc = sum(jnp.dot(tri, part, preferred_element_type=F32) for part in _split3(G))
        GT = G.T
        BT = Bc.T
        base = 0 if fwd else 2 * ML_HEADS
        for hh in range(ML_HEADS):
            ci, cf = base + hh, base + ML_HEADS + hh
            b_col = Bc[:, cf:cf + 1]
            b_last = b_col[L - 1:L, :] if fwd else b_col[0:1, :]
            sl = slice(hh * ML_HEAD_DIM, (hh + 1) * ML_HEAD_DIM)
            h, C, n, m = _mlstm_chunk(
                q_ref[0, :, sl], k_ref[0, :, sl], v_ref[0, :, sl],
                b_col, BT[cf:cf + 1, :], G[:, ci:ci + 1], GT[ci:ci + 1, :], b_last, mask,
                c_scr[hh], n_scr[hh], m_scr[hh][:, 0:1])
            h_ref[0, :, sl] = h
            c_scr[hh] = C
            n_scr[hh] = n
            m_scr[hh] = jnp.broadcast_to(m, (1, LANES))

    run(True, qf_ref, kf_ref, vf_ref, gf_ref, hf_ref, cf_scr, nf_scr, mf_scr)
    run(False, qb_ref, kb_ref, vb_ref, gb_ref, hb_ref, cb_scr, nb_scr, mb_scr)

    @pl.when(i == pl.num_programs(1) - 1)
    def _():
        c1f_ref[0] = cf_scr[...]
        n1f_ref[0] = nf_scr[...]
        m1f_ref[0] = mf_scr[...]
        c1b_ref[0] = cb_scr[...]
        n1b_ref[0] = nb_scr[...]
        m1b_ref[0] = mb_scr[...]


def _mlstm(z, zg, bias, state):
    B, T, _ = z.shape
    L = ML_CHUNK
    nc = T // L
    H, dh = ML_HEADS, ML_HEAD_DIM
    bq, bk, bv = COL_MQ // BRANCH_W, COL_MK // BRANCH_W, COL_MV // BRANCH_W
    fw = lambda blk: (lambda b, i: (b, i, blk))
    bw = lambda blk: (lambda b, i: (b, nc - 1 - i, blk))
    st4 = lambda b, i: (b, 0, 0, 0)
    c_spec = pl.BlockSpec((1, H, dh, dh), st4)
    v_spec = pl.BlockSpec((1, H, 1, LANES), st4)
    c_shape = jax.ShapeDtypeStruct((B, H, dh, dh), F32)
    v_shape = jax.ShapeDtypeStruct((B, H, 1, LANES), F32)
    tok = lambda idx: pl.BlockSpec((1, L, BRANCH_W), idx)
    gsp = lambda idx: pl.BlockSpec((1, L, LANES), idx)
    outs = pl.pallas_call(
        _mlstm_body,
        grid=(B, nc),
        in_specs=[tok(fw(bq)), tok(fw(bk)), tok(fw(bv)), gsp(fw(0)),
                  tok(bw(bq)), tok(bw(bk)), tok(bw(bv)), gsp(bw(0)),
                  pl.BlockSpec((1, LANES), lambda b, i: (0, 0)),
                  c_spec, v_spec, v_spec, c_spec, v_spec, v_spec],
        out_specs=[tok(fw(0)), tok(bw(0)), c_spec, v_spec, v_spec, c_spec, v_spec, v_spec],
        out_shape=[jax.ShapeDtypeStruct((B, T, BRANCH_W), F32)] * 2 + [c_shape, v_shape, v_shape] * 2,
        scratch_shapes=[pltpu.VMEM((H, dh, dh), F32), pltpu.VMEM((H, 1, LANES), F32), pltpu.VMEM((H, 1, LANES), F32)] * 2,
        compiler_params=_cparams("parallel", "arbitrary"),
        name="mlstm_scan",
    )(z, z, z, zg, z, z, z, zg, bias, *state)
    return outs[0], outs[1], tuple(outs[2:])


def _qkprep_body(rope, q_ref, k_ref, gq_ref, gk_ref, *rest):
    if rope:
        cos_ref, sa_ref, sb_ref, qo_ref, ko_ref = rest
    else:
        qo_ref, ko_ref = rest
    tm = q_ref.shape[1]
    r = lax.broadcasted_iota(jnp.int32, (LANES, LANES), 0) // HEAD_DIM
    c = lax.broadcasted_iota(jnp.int32, (LANES, LANES), 1) // HEAD_DIM
    head_ones = jnp.where(r == c, 1.0, 0.0).astype(BF16)
    lo = lax.broadcasted_iota(jnp.int32, (tm, LANES), 1) < HEAD_DIM

    def norm_rope(x, g):
        ss = jnp.dot((x * x).astype(BF16), head_ones, preferred_element_type=F32)
        y = x * lax.rsqrt(ss * (1.0 / HEAD_DIM) + EPS) * g
        if rope:
            y = y * cos_ref[...] + pltpu.roll(y, LANES - 16, 1) * sa_ref[...] + pltpu.roll(y, 16, 1) * sb_ref[...]
        return y

    for j in range(BRANCH_W // LANES):
        y = norm_rope(q_ref[0, :, j * LANES:(j + 1) * LANES].astype(F32), gq_ref[...]) * (HEAD_DIM ** -0.5)
        even = jnp.where(lo, y, 0.0)
        odd = jnp.where(lo, 0.0, y)
        if j // 2 == 0:
            odd = pltpu.roll(odd, HEAD_DIM, 1)
        else:
            even = pltpu.roll(even, HEAD_DIM, 1)
        qo_ref[0, :, (2 * j) * LANES:(2 * j + 1) * LANES] = even.astype(BF16)
        qo_ref[0, :, (2 * j + 1) * LANES:(2 * j + 2) * LANES] = odd.astype(BF16)
    ko_ref[0] = norm_rope(k_ref[0].astype(F32), gk_ref[...]).astype(BF16)


def _qkprep(z, gq, gk, tables, tm):
    B, T, _ = z.shape
    rope = tables is not None
    in_specs = [pl.BlockSpec((1, tm, BRANCH_W), lambda b, i: (b, i, COL_AQ // BRANCH_W)),
                pl.BlockSpec((1, tm, LANES), lambda b, i: (b, i, COL_AK // LANES)),
                pl.BlockSpec((1, LANES), lambda b, i: (0, 0)),
                pl.BlockSpec((1, LANES), lambda b, i: (0, 0))]
    args = [z, z, gq, gk]
    if rope:
        in_specs += [pl.BlockSpec((tm, LANES), lambda b, i: (i, 0))] * 3
        args += list(tables)
    return pl.pallas_call(
        functools.partial(_qkprep_body, rope),
        grid=(B, T // tm),
        in_specs=in_specs,
        out_specs=[pl.BlockSpec((1, tm, 2 * BRANCH_W), lambda b, i: (b, i, 0)),
                   pl.BlockSpec((1, tm, LANES), lambda b, i: (b, i, 0))],
        out_shape=[jax.ShapeDtypeStruct((B, T, 2 * BRANCH_W), BF16), jax.ShapeDtypeStruct((B, T, LANES), BF16)],
        compiler_params=_cparams("parallel", "parallel"),
        name="qk_prep",
    )(*args)


def _rope_tables(n_tok):
    half = HEAD_DIM // 2
    t = jnp.arange(n_tok)
    inv = ROPE_THETA ** (-jnp.arange(0, half, 2, dtype=F32) / half)
    ang_r = (t // GRID_W).astype(F32)[:, None] * inv
    ang_c = (t % GRID_W).astype(F32)[:, None] * inv
    ang = jnp.concatenate([ang_r, ang_r, ang_c, ang_c], axis=-1)
    cos, sin = jnp.cos(ang), jnp.sin(ang)
    even_quarter = ((np.arange(HEAD_DIM) // (HEAD_DIM // 4)) % 2 == 0)[None, :]
    sa = jnp.where(even_quarter, -sin, 0.0)
    sb = jnp.where(even_quarter, 0.0, sin)
    tile = lambda a: jnp.concatenate([a, a], axis=-1)
    return tile(cos), tile(sa), tile(sb)


def _gqa_body(tk, q_ref, k_ref, v_ref, o_ref):
    g = pl.program_id(1)
    tq = q_ref.shape[1]
    S = k_ref.shape[1]
    n_heads = BRANCH_W // LANES
    q = jnp.concatenate([q_ref[0, :, h * LANES:(h + 1) * LANES] for h in range(n_heads)], axis=0)
    bounds = [0] + list(range(S % tk, S, tk)) if S % tk else list(range(0, S, tk))
    m = l = acc = None
    for off, end in zip(bounds, bounds[1:] + [S]):
        k = k_ref[0, off:end, :]
        v = v_ref[0, off:end, :]
        s = lax.dot_general(q, k, _NT, preferred_element_type=F32)
        mc = jnp.max(s, axis=-1, keepdims=True)
        if m is None:
            m = mc
            p = jnp.exp(s - m)
            l = jnp.sum(p, axis=-1, keepdims=True)
            acc = jnp.dot(p.astype(BF16), v, preferred_element_type=F32)
        else:
            m_new = jnp.maximum(m, mc)
            alpha = jnp.exp(m - m_new)
            p = jnp.exp(s - m_new)
            l = alpha * l + jnp.sum(p, axis=-1, keepdims=True)
            acc = alpha * acc + jnp.dot(p.astype(BF16), v, preferred_element_type=F32)
            m = m_new
    o = acc / l
    lo = lax.broadcasted_iota(jnp.int32, (tq, LANES), 1) < HEAD_DIM
    first = g == 0
    for j in range(n_heads // 2):
        a = o[(2 * j) * tq:(2 * j + 1) * tq]
        b = o[(2 * j + 1) * tq:(2 * j + 2) * tq]
        pair0 = jnp.where(lo, a, pltpu.roll(b, HEAD_DIM, 1))
        pair1 = jnp.where(lo, pltpu.roll(a, HEAD_DIM, 1), b)
        o_ref[0, :, j * LANES:(j + 1) * LANES] = jnp.where(first, pair0, pair1).astype(BF16)


def _gqa(qp, k_all, v_all, tq, tk):
    B, T, _ = qp.shape
    S = k_all.shape[1]
    n_kv = 2
    return pl.pallas_call(
        functools.partial(_gqa_body, tk),
        grid=(B, n_kv, T // tq),
        in_specs=[pl.BlockSpec((1, tq, BRANCH_W), lambda b, g, i: (b, i, g)),
                  pl.BlockSpec((1, S, LANES), lambda b, g, i: (b, 0, 0)),
                  pl.BlockSpec((1, S, LANES), lambda b, g, i: (b, 0, 0))],
        out_specs=pl.BlockSpec((1, tq, BRANCH_W // n_kv), lambda b, g, i: (b, i, g)),
        out_shape=jax.ShapeDtypeStruct((B, T, BRANCH_W), BF16),
        compiler_params=_cparams("parallel", "parallel", "parallel"),
        name="gqa_flash",
    )(qp, k_all, v_all)


def _na_body(has_win, wk, rows, q_ref, *rest):
    if has_win:
        k_ref, v_ref, kc_ref, vc_ref, bm_ref, o_ref = rest
    else:
        kc_ref, vc_ref, o_ref = rest
    i = pl.program_id(1)
    tq = q_ref.shape[1]
    lo = lax.broadcasted_iota(jnp.int32, (tq, LANES), 1) < HEAD_DIM
    if has_win:
        ks = jnp.clip(i * NA_ROWS - NA_WIN_R // 2, 0, rows - wk // GRID_W)
        off = pl.multiple_of(ks * GRID_W, GRID_W)
    for j in range(BRANCH_W // LANES):
        sl = slice(j * LANES, (j + 1) * LANES)
        q2 = q_ref[0, :, sl] * (HEAD_DIM ** -0.5)
        kc = kc_ref[0, :, sl]
        vc = vc_ref[0, :, sl]
        if has_win:
            kw = k_ref[0, pl.ds(off, wk), sl]
            vw = v_ref[0, pl.ds(off, wk), sl]
        halves = []
        for e in range(2):
            qh = jnp.where(lo if e == 0 else jnp.logical_not(lo), q2, jnp.zeros_like(q2))
            sc = lax.dot_general(qh, kc, _NT, preferred_element_type=F32)
            m = jnp.max(sc, axis=-1, keepdims=True)
            if has_win:
                sw = lax.dot_general(qh, kw, _NT, preferred_element_type=F32) + bm_ref[0, 2 * j + e]
                m = jnp.maximum(m, jnp.max(sw, axis=-1, keepdims=True))
            pc = jnp.exp(sc - m)
            l = jnp.sum(pc, axis=-1, keepdims=True)
            acc = jnp.dot(pc.astype(BF16), vc, preferred_element_type=F32)
            if has_win:
                pw = jnp.exp(sw - m)
                l = l + jnp.sum(pw, axis=-1, keepdims=True)
                acc = acc + jnp.dot(pw.astype(BF16), vw, preferred_element_type=F32)
            halves.append(acc / l)
        o_ref[0, :, sl] = jnp.where(lo, halves[0], halves[1]).astype(BF16)


def _na(z, zc, bm):
    B, T, _ = z.shape
    Tc = zc.shape[1]
    rows = T // GRID_W
    tq = NA_ROWS * GRID_W
    wk = (NA_ROWS + NA_WIN_R - 1) * GRID_W
    nb = rows // NA_ROWS
    bq, bk, bv = COL_NQ // BRANCH_W, COL_NK // BRANCH_W, COL_NV // BRANCH_W

    def bm_index(b, i):
        return (jnp.where(i == 0, 0, jnp.where(i == nb - 1, 2, 1)), 0, 0, 0)

    return pl.pallas_call(
        functools.partial(_na_body, True, wk, rows),
        grid=(B, nb),
        in_specs=[pl.BlockSpec((1, tq, BRANCH_W), lambda b, i: (b, i, bq)),
                  pl.BlockSpec((1, T, BRANCH_W), lambda b, i: (b, 0, bk)),
                  pl.BlockSpec((1, T, BRANCH_W), lambda b, i: (b, 0, bv)),
                  pl.BlockSpec((1, Tc, BRANCH_W), lambda b, i: (b, 0, bk)),
                  pl.BlockSpec((1, Tc, BRANCH_W), lambda b, i: (b, 0, bv)),
                  pl.BlockSpec((1, 2 * BRANCH_W // LANES, tq, wk), bm_index)],
        out_specs=pl.BlockSpec((1, tq, BRANCH_W), lambda b, i: (b, i, 0)),
        out_shape=jax.ShapeDtypeStruct((B, T, BRANCH_W), BF16),
        compiler_params=_cparams("parallel", "arbitrary"),
        name="na_window",
    )(z, z, z, zc, zc, bm)


def _na_ctx(zc):
    B, Tc, _ = zc.shape
    bq, bk, bv = COL_NQ // BRANCH_W, COL_NK // BRANCH_W, COL_NV // BRANCH_W
    return pl.pallas_call(
        functools.partial(_na_body, False, 0, 0),
        grid=(B, 1),
        in_specs=[pl.BlockSpec((1, Tc, BRANCH_W), lambda b, i: (b, 0, bq)),
                  pl.BlockSpec((1, Tc, BRANCH_W), lambda b, i: (b, 0, bk)),
                  pl.BlockSpec((1, Tc, BRANCH_W), lambda b, i: (b, 0, bv))],
        out_specs=pl.BlockSpec((1, Tc, BRANCH_W), lambda b, i: (b, 0, 0)),
        out_shape=jax.ShapeDtypeStruct((B, Tc, BRANCH_W), BF16),
        compiler_params=_cparams("parallel", "arbitrary"),
        name="na_context",
    )(zc, zc, zc)


def _na_bias_tables(rpb, rows):
    R, W = NA_ROWS, GRID_W
    wr, wc = min(NA_WIN_R, rows), NA_WIN_C
    nkr = R + NA_WIN_R - 1
    nb = rows // R
    depth, heads = rpb.shape[:2]
    c = np.arange(W)[:, None]
    kc = np.arange(W)[None, :]
    c0 = np.clip(c - wc // 2, 0, W - wc)
    col_ok = (kc >= c0) & (kc < c0 + wc)
    onehot = ((kc - c + (NA_WIN_C - 1))[None] == np.arange(2 * NA_WIN_C - 1)[:, None, None]) & col_ok[None]
    toep = jnp.einsum('lhrd,dcx->lhrcx', rpb, jnp.asarray(onehot, F32), precision=lax.Precision.HIGHEST)
    toep = jnp.where(col_ok, toep, NEG)
    masked = jnp.full((depth, heads, W, W), NEG, F32)
    kinds = []
    for blk in (0, 1, nb - 1):
        rb = blk * R
        ks = int(np.clip(rb - NA_WIN_R // 2, 0, rows - nkr))
        q_rows = []
        for ri in range(R):
            r = rb + ri
            r0 = int(np.clip(r - wr // 2, 0, rows - wr))
            tiles = [toep[:, :, a - r + (NA_WIN_R - 1)] if r0 <= a < r0 + wr else masked for a in range(ks, ks + nkr)]
            q_rows.append(jnp.concatenate(tiles, axis=-1))
        kinds.append(jnp.concatenate(q_rows, axis=-2))
    return jnp.stack(kinds, axis=1)


def _merge_body(hf_ref, hb_ref, zo_ref, zg_ref, ga_ref, na_ref, x_ref, gml_ref, wb_ref, wo_ref, gpost_ref, mod_ref, o_ref):
    h = hf_ref[0] + hb_ref[0]
    hn = jnp.concatenate([_rms(h[:, t * ML_HEAD_DIM:(t + 1) * ML_HEAD_DIM]) for t in range(ML_HEADS)], axis=-1)
    y_ml = (hn * gml_ref[...] * _sigmoid(zo_ref[0].astype(F32))).astype(BF16)
    acc = None
    for n, y in enumerate((y_ml, ga_ref[0], na_ref[0])):
        gate = _sigmoid(zg_ref[0, :, n * D_MODEL:(n + 1) * D_MODEL].astype(F32))
        term = gate * jnp.dot(y, wb_ref[n], preferred_element_type=F32)
        acc = term if acc is None else acc + term
    o = jnp.dot(acc.astype(BF16), wo_ref[...], preferred_element_type=F32)
    o_ref[0] = x_ref[0] + mod_ref[0, 2:3, :] * (_rms(o) * gpost_ref[...])


def _merge(hf, hb, z, y_ga, y_na, x, g_ml, wb, wo, g_post, mod, tm):
    B, T, d = x.shape
    tok = lambda w, blk: pl.BlockSpec((1, tm, w), lambda b, i: (b, i, blk))
    full = lambda a: pl.BlockSpec(a.shape, lambda b, i: (0,) * a.ndim)
    g_ml = g_ml.reshape(1, BRANCH_W)
    g_post = g_post.reshape(1, d)
    return pl.pallas_call(
        _merge_body,
        grid=(B, T // tm),
        in_specs=[tok(BRANCH_W, 0), tok(BRANCH_W, 0), tok(BRANCH_W, COL_MO // BRANCH_W), tok(3 * d, 0),
                  tok(BRANCH_W, 0), tok(BRANCH_W, 0), tok(d, 0),
                  full(g_ml), full(wb), full(wo), full(g_post),
                  pl.BlockSpec((1, 6, d), _mod_index(mod))],
        out_specs=tok(d, 0),
        out_shape=jax.ShapeDtypeStruct((B, T, d), F32),
        compiler_params=_cparams("parallel", "parallel"),
        name="merge_out",
    )(hf, hb, z, z, y_ga, y_na, x, g_ml, wb, wo, g_post, mod)


def _ffn_body(x_ref, xp_ref, xn_ref, g_ref, mod_ref, wa_ref, wg_ref, cwa_ref, cwg_ref, cba_ref, cbg_ref, wd_ref, gpost_ref,
              o_ref, h_scr, halo_scr, acc_scr):
    i = pl.program_id(1)
    f = pl.program_id(2)
    tm = x_ref.shape[1]
    edge = xp_ref.shape[1]

    @pl.when(f == 0)
    def _():
        def nm(x):
            return (_rms(x) * g_ref[...] * (1.0 + mod_ref[0, 4:5, :]) + mod_ref[0, 3:4, :]).astype(BF16)
        h_scr[...] = nm(x_ref[0])
        halo_scr[0:edge] = nm(xp_ref[0])
        halo_scr[edge:2 * edge] = nm(xn_ref[0])
        acc_scr[...] = jnp.zeros(acc_scr.shape, F32)

    h2 = h_scr[...]
    halo = halo_scr[...]
    has_prev = i > 0
    has_next = i < pl.num_programs(1) - 1

    def conv_branch(w_ref, cw_ref, cb_ref):
        w = w_ref[...]
        u = jnp.dot(h2, w, preferred_element_type=F32)
        uh = jnp.dot(halo, w, preferred_element_type=F32)
        prev_row = jnp.where(has_prev, uh[edge - 1:edge], 0.0)
        next_row = jnp.where(has_next, uh[edge:edge + 1], 0.0)
        r = lax.broadcasted_iota(jnp.int32, u.shape, 0)
        u_prev = jnp.where(r == 0, prev_row, pltpu.roll(u, 1, 0))
        u_next = jnp.where(r == tm - 1, next_row, pltpu.roll(u, tm - 1, 0))
        return u_prev * cw_ref[0:1, :] + u * cw_ref[1:2, :] + u_next * cw_ref[2:3, :] + cb_ref[...]

    a = conv_branch(wa_ref, cwa_ref, cba_ref)
    g = conv_branch(wg_ref, cwg_ref, cbg_ref)
    act = (a * (g * _sigmoid(g))).astype(BF16)
    acc_scr[...] += jnp.dot(act, wd_ref[...], preferred_element_type=F32)

    @pl.when(f == pl.num_programs(2) - 1)
    def _():
        o_ref[0] = x_ref[0] + mod_ref[0, 5:6, :] * (_rms(acc_scr[...]) * gpost_ref[...])


def _ffn(x, g_pre, mod, w_up, cw, cb, w_down, g_post, tm, tf):
    B, T, d = x.shape
    edge = 8
    nf = FFN_DIM // tf
    per = tm // edge
    last_edge = T // edge - 1
    return pl.pallas_call(
        _ffn_body,
        grid=(B, T // tm, nf),
        in_specs=[pl.BlockSpec((1, tm, d), lambda b, i, f: (b, i, 0)),
                  pl.BlockSpec((1, edge, d), lambda b, i, f: (b, jnp.maximum(i * per - 1, 0), 0)),
                  pl.BlockSpec((1, edge, d), lambda b, i, f: (b, jnp.minimum((i + 1) * per, last_edge), 0)),
                  pl.BlockSpec((1, d), lambda b, i, f: (0, 0)),
                  pl.BlockSpec((1, 6, d), _mod_index(mod)),
                  pl.BlockSpec((d, tf), lambda b, i, f: (0, f)),
                  pl.BlockSpec((d, tf), lambda b, i, f: (0, nf + f)),
                  pl.BlockSpec((3, tf), lambda b, i, f: (0, f)),
                  pl.BlockSpec((3, tf), lambda b, i, f: (0, nf + f)),
                  pl.BlockSpec((1, tf), lambda b, i, f: (0, f)),
                  pl.BlockSpec((1, tf), lambda b, i, f: (0, nf + f)),
                  pl.BlockSpec((tf, d), lambda b, i, f: (f, 0)),
                  pl.BlockSpec((1, d), lambda b, i, f: (0, 0))],
        out_specs=pl.BlockSpec((1, tm, d), lambda b, i, f: (b, i, 0)),
        out_shape=jax.ShapeDtypeStruct((B, T, d), F32),
        scratch_shapes=[pltpu.VMEM((tm, d), BF16), pltpu.VMEM((2 * edge, d), BF16), pltpu.VMEM((tm, d), F32)],
        compiler_params=_cparams("parallel", "parallel", "arbitrary"),
        name="conv_ffn",
    )(x, x, x, g_pre.reshape(1, d), mod, w_up, w_up, cw, cw, cb.reshape(1, -1), cb.reshape(1, -1), w_down, g_post.reshape(1, d))


def _prep_w_in(w_in):
    sizes = (512, 512, 512, 512, 16, 512, 128, 128, 512, 512, 512, 3072)
    offs = np.concatenate([[0], np.cumsum(sizes)])
    seg = lambda n: w_in[:, :, offs[n]:offs[n + 1]]
    mq, mk, mv, mo, mg, aq, ak, av, nq, nk, nv, zg = (seg(n) for n in range(12))
    mk = mk * (ML_HEAD_DIM ** -0.5)
    main = jnp.concatenate([zg, mq, mk, mv, mo, aq, nq, nk, nv, ak, av], axis=-1).astype(BF16)
    gate = jnp.pad(mg, ((0, 0), (0, 0), (0, LANES - mg.shape[-1]))).astype(BF16)
    return main, gate


def kernel(x, c, ctx, c_ctx, w_mod, b_mod, g_pre_mix, g_post_mix, g_pre_ffn, g_post_ffn, w_in, b_ml_gates, g_ml_out, g_q, g_k, na_rpb, w_branch, w_out, w_up, conv_w, conv_b, w_down):
    B, T, d = x.shape
    Tc = ctx.shape[1]
    depth = w_mod.shape[0]
    rows = T // GRID_W

    w_main, w_gate = _prep_w_in(w_in)
    wb, wo, wu, wd = (a.astype(BF16) for a in (w_branch, w_out, w_up, w_down))
    n_mod = -(-(B + 1) // 8) * 8
    cc = jnp.concatenate([c, c_ctx[None], jnp.zeros((n_mod - B - 1, d), F32)], axis=0)
    mod_all = _mod_call(cc, w_mod, b_mod)
    tables = _rope_tables(T)
    bias_tables = _na_bias_tables(na_rpb, rows)
    gate_bias = jnp.pad(b_ml_gates, ((0, 0), (0, LANES - b_ml_gates.shape[-1])))
    gq2 = jnp.concatenate([g_q, g_q], axis=-1)
    gk2 = jnp.concatenate([g_k, g_k], axis=-1)
    H, dh = ML_HEADS, ML_HEAD_DIM
    zero_state = (jnp.zeros((B, H, dh, dh), F32), jnp.zeros((B, H, 1, LANES), F32), jnp.zeros((B, H, 1, LANES), F32)) * 2

    tm = min(512, T)
    tq = min(128, T)
    xc = ctx
    for l in range(depth):
        last = l == depth - 1
        mod_l = mod_all[l, :B].reshape(B, 6, d)
        mod_c = mod_all[l, B:B + 1].reshape(1, 6, d)
        bias_l = gate_bias[l][None]

        z, zg = _inproj(x, g_pre_mix[l], mod_l, w_main[l], w_gate[l], tm)
        zc, zgc = _inproj(xc, g_pre_mix[l], mod_c, w_main[l], w_gate[l], Tc)

        hfc, hbc, state = _mlstm(zc, zgc, bias_l, zero_state)
        hf, hb, _ = _mlstm(z, zg, bias_l, state)

        qp, kn = _qkprep(z, gq2[l][None], gk2[l][None], tables, tm)
        qpc, knc = _qkprep(zc, gq2[l][None], gk2[l][None], None, Tc)
        vc_a = zc[:, :, COL_AV:COL_AV + LANES]
        k_all = jnp.concatenate([knc, kn], axis=1)
        v_all = jnp.concatenate([vc_a, z[:, :, COL_AV:COL_AV + LANES]], axis=1)
        y_ga = _gqa(qp, k_all, v_all, tq, GQA_KEY_CHUNK)
        y_na = _na(z, zc, bias_tables[l])

        x_mid = _merge(hf, hb, z, y_ga, y_na, x, g_ml_out[l], wb[l], wo[l], g_post_mix[l], mod_l, tm)
        x_new = _ffn(x_mid, g_pre_ffn[l], mod_l, wu[l], conv_w[l], conv_b[l], wd[l], g_post_ffn[l], tm, FFN_DIM // 2)

        if not last:
            yc_ga = _gqa(qpc, knc, vc_a, min(128, Tc), Tc)
            yc_na = _na_ctx(zc)
            xc_mid = _merge(hfc, hbc, zc, yc_ga, yc_na, xc, g_ml_out[l], wb[l], wo[l], g_post_mix[l], mod_c, Tc)
            xc = _ffn(xc_mid, g_pre_ffn[l], mod_c, wu[l], conv_w[l], conv_b[l], wd[l], g_post_ffn[l], Tc, FFN_DIM // 2)
        x = x_new
    return x
```

```python
import functools

import numpy as np
import jax
import jax.numpy as jnp
from jax import lax
from jax.experimental import pallas as pl
from jax.experimental.pallas import tpu as pltpu

F32 = jnp.float32
BF16 = jnp.bfloat16

D_MODEL = 1024
BRANCH_W = 512
HEAD_DIM = 64
ML_HEADS = 4
ML_HEAD_DIM = 128
GRID_W = 64
NA_WIN_R = 8
NA_WIN_C = 16
ROPE_THETA = 10000.0
FFN_DIM = 2816
EPS = 1e-6
LANES = 128
ML_CHUNK = 256
NA_ROWS = 4
GQA_KEY_CHUNK = 256
GQA_TQ = 256
TOKEN_TILE = 512
FFN_TM = 512
FFN_TF = FFN_DIM
NEG = -1e30
LOG2E = 1.4426950408889634
ATTN_SCALE = HEAD_DIM ** -0.5 * LOG2E
VMEM_LIMIT = 56 * 1024 * 1024

NZ = 7424
COL_ZG = 0
COL_MQ, COL_MK, COL_MV, COL_MO = 3072, 3584, 4096, 4608
COL_AQ = 5120
COL_NQ, COL_NK, COL_NV = 5632, 6144, 6656
COL_AK, COL_AV = 7168, 7296

_NT = (((1,), (1,)), ((), ()))
_TN = (((0,), (0,)), ((), ()))


def _cparams(*sem):
    return pltpu.CompilerParams(dimension_semantics=sem, vmem_limit_bytes=VMEM_LIMIT)


def _rms(x):
    return x * lax.rsqrt(jnp.mean(x * x, axis=-1, keepdims=True) + EPS)


def _sigmoid(x):
    return 1.0 / (1.0 + jnp.exp(-x))


def _mod_body(c_ref, w_ref, b_ref, o_ref):
    c = c_ref[...]
    s = (c * _sigmoid(c)).astype(BF16)
    o_ref[0] = jnp.dot(s, w_ref[0].astype(BF16), preferred_element_type=F32) + b_ref[0]


def _mod_call(cc, w_mod, b_mod):
    depth, d, n = w_mod.shape
    tn = 1536
    return pl.pallas_call(
        _mod_body,
        grid=(depth, n // tn),
        in_specs=[pl.BlockSpec((cc.shape[0], d), lambda l, j: (0, 0)),
                  pl.BlockSpec((1, d, tn), lambda l, j: (l, 0, j)),
                  pl.BlockSpec((1, 1, tn), lambda l, j: (l, 0, j))],
        out_specs=pl.BlockSpec((1, cc.shape[0], tn), lambda l, j: (l, 0, j)),
        out_shape=jax.ShapeDtypeStruct((depth, cc.shape[0], n), F32),
        compiler_params=_cparams("parallel", "parallel"),
        name="modulation",
    )(cc, w_mod, b_mod.reshape(depth, 1, n))


def _inproj_body(x_ref, g_ref, mod_ref, w_ref, wg_ref, z_ref, zg_ref, h_scr):
    @pl.when(pl.program_id(2) == 0)
    def _():
        y = _rms(x_ref[0]) * g_ref[...]
        h = (y * (1.0 + mod_ref[0, 1:2, :]) + mod_ref[0, 0:1, :]).astype(BF16)
        h_scr[...] = h
        zg_ref[0] = jnp.dot(h, wg_ref[...], preferred_element_type=F32)

    z_ref[0] = jnp.dot(h_scr[...], w_ref[...], preferred_element_type=F32).astype(BF16)


def _mod_index(mod):
    if mod.shape[0] > 1:
        return lambda b, *_: (b, 0, 0)
    return lambda b, *_: (0, 0, 0)


def _inproj(x, g, mod, w, wg, tm):
    B, T, d = x.shape
    tn = NZ // 2
    return pl.pallas_call(
        _inproj_body,
        grid=(B, T // tm, NZ // tn),
        in_specs=[pl.BlockSpec((1, tm, d), lambda b, i, j: (b, i, 0)),
                  pl.BlockSpec((1, d), lambda b, i, j: (0, 0)),
                  pl.BlockSpec((1, 6, d), _mod_index(mod)),
                  pl.BlockSpec((d, tn), lambda b, i, j: (0, j)),
                  pl.BlockSpec((d, LANES), lambda b, i, j: (0, 0))],
        out_specs=[pl.BlockSpec((1, tm, tn), lambda b, i, j: (b, i, j)),
                   pl.BlockSpec((1, tm, LANES), lambda b, i, j: (b, i, 0))],
        out_shape=[jax.ShapeDtypeStruct((B, T, NZ), BF16), jax.ShapeDtypeStruct((B, T, LANES), F32)],
        scratch_shapes=[pltpu.VMEM((tm, d), BF16)],
        compiler_params=_cparams("parallel", "parallel", "arbitrary"),
        name="inproj",
    )(x, g.reshape(1, d), mod, w, wg)


def _log_sigmoid(x):
    return jnp.minimum(x, 0.0) - jnp.log1p(jnp.exp(-jnp.abs(x)))


def _split3(a):
    a1 = a.astype(BF16)
    r1 = a - a1.astype(F32)
    a2 = r1.astype(BF16)
    a3 = (r1 - a2.astype(F32)).astype(BF16)
    return a1, a2, a3


def _mlstm_chunk(q, k, v, b_col, b_row, ig_col, ig_row, b_last, mask, C, n, m):
    s0 = lax.dot_general(q, k, _NT, preferred_element_type=F32)
    logd = jnp.where(mask, b_col - b_row + ig_row, NEG)
    bm = b_col + m
    m_t = jnp.maximum(bm, jnp.max(logd, axis=-1, keepdims=True))
    s = s0 * jnp.exp(logd - m_t)
    inter = jnp.exp(bm - m_t)
    q_c = jnp.dot(q, C.astype(BF16), preferred_element_type=F32)
    num = jnp.dot(s.astype(BF16), v, preferred_element_type=F32) + inter * q_c
    q_n = jnp.sum(q.astype(F32) * n, axis=-1, keepdims=True)
    den = jnp.sum(s, axis=-1, keepdims=True) + inter * q_n
    h = num / jnp.maximum(jnp.abs(den), jnp.exp(-m_t))

    lw = b_last - b_col + ig_col
    m_new = jnp.maximum(b_last + m, jnp.max(lw, axis=0, keepdims=True))
    w = jnp.exp(lw - m_new)
    decay = jnp.exp(b_last + m - m_new)
    wv = (w * v.astype(F32)).astype(BF16)
    C_new = decay * C + lax.dot_general(k, wv, _TN, preferred_element_type=F32)
    n_new = decay * n + jnp.sum(w * k.astype(F32), axis=0, keepdims=True)
    return h, C_new, n_new, m_new


def _mlstm_body(qf_ref, kf_ref, vf_ref, gf_ref, qb_ref, kb_ref, vb_ref, gb_ref, bias_ref,
                c0f_ref, n0f_ref, m0f_ref, c0b_ref, n0b_ref, m0b_ref,
                hf_ref, hb_ref, c1f_ref, n1f_ref, m1f_ref, c1b_ref, n1b_ref, m1b_ref,
                cf_scr, nf_scr, mf_scr, cb_scr, nb_scr, mb_scr):
    i = pl.program_id(1)
    L = ML_CHUNK

    @pl.when(i == 0)
    def _():
        cf_scr[...] = c0f_ref[0]
        nf_scr[...] = n0f_ref[0]
        mf_scr[...] = m0f_ref[0]
        cb_scr[...] = c0b_ref[0]
        nb_scr[...] = n0b_ref[0]
        mb_scr[...] = m0b_ref[0]

    row = lax.broadcasted_iota(jnp.int32, (L, L), 0)
    col = lax.broadcasted_iota(jnp.int32, (L, L), 1)
    lane = lax.broadcasted_iota(jnp.int32, (L, LANES), 1)
    is_forget = (lane % 8) >= 4

    def run(fwd, q_ref, k_ref, v_ref, g_ref, h_ref, c_scr, n_scr, m_scr):
        mask = (row >= col) if fwd else (row <= col)
        tri = jnp.where(mask, 1.0, 0.0).astype(BF16)
        g = g_ref[0] + bias_ref[...]
        G = jnp.where(is_forget, _log_sigmoid(g), g)
        Bc = sum(jnp.dot(tri, part, preferred_element_type=F32) for part in _split3(G))
        GT = G.T
        BT = Bc.T
        base = 0 if fwd else 2 * ML_HEADS
        for hh in range(ML_HEADS):
            ci, cf = base + hh, base + ML_HEADS + hh
            b_col = Bc[:, cf:cf + 1]
            b_last = b_col[L - 1:L, :] if fwd else b_col[0:1, :]
            sl = slice(hh * ML_HEAD_DIM, (hh + 1) * ML_HEAD_DIM)
            h, C, n, m = _mlstm_chunk(
                q_ref[0, :, sl], k_ref[0, :, sl], v_ref[0, :, sl],
                b_col, BT[cf:cf + 1, :], G[:, ci:ci + 1], GT[ci:ci + 1, :], b_last, mask,
                c_scr[hh], n_scr[hh], m_scr[hh][:, 0:1])
            h_ref[0, :, sl] = h
            c_scr[hh] = C
            n_scr[hh] = n
            m_scr[hh] = jnp.broadcast_to(m, (1, LANES))

    run(True, qf_ref, kf_ref, vf_ref, gf_ref, hf_ref, cf_scr, nf_scr, mf_scr)
    run(False, qb_ref, kb_ref, vb_ref, gb_ref, hb_ref, cb_scr, nb_scr, mb_scr)

    @pl.when(i == pl.num_programs(1) - 1)
    def _():
        c1f_ref[0] = cf_scr[...]
        n1f_ref[0] = nf_scr[...]
        m1f_ref[0] = mf_scr[...]
        c1b_ref[0] = cb_scr[...]
        n1b_ref[0] = nb_scr[...]
        m1b_ref[0] = mb_scr[...]


def _zero_state(B):
    H, dh = ML_HEADS, ML_HEAD_DIM
    return (jnp.zeros((B, H, dh, dh), F32), jnp.zeros((B, H, 1, LANES), F32), jnp.zeros((B, H, 1, LANES), F32)) * 2


def _mlstm(z, zg, bias, state):
    B, T, _ = z.shape
    L = ML_CHUNK
    nc = T // L
    H, dh = ML_HEADS, ML_HEAD_DIM
    bq, bk, bv = COL_MQ // BRANCH_W, COL_MK // BRANCH_W, COL_MV // BRANCH_W
    fw = lambda blk: (lambda b, i: (b, i, blk))
    bw = lambda blk: (lambda b, i: (b, nc - 1 - i, blk))
    st4 = lambda b, i: (b, 0, 0, 0)
    c_spec = pl.BlockSpec((1, H, dh, dh), st4)
    v_spec = pl.BlockSpec((1, H, 1, LANES), st4)
    c_shape = jax.ShapeDtypeStruct((B, H, dh, dh), F32)
    v_shape = jax.ShapeDtypeStruct((B, H, 1, LANES), F32)
    tok = lambda idx: pl.BlockSpec((1, L, BRANCH_W), idx)
    gsp = lambda idx: pl.BlockSpec((1, L, LANES), idx)
    outs = pl.pallas_call(
        _mlstm_body,
        grid=(B, nc),
        in_specs=[tok(fw(bq)), tok(fw(bk)), tok(fw(bv)), gsp(fw(0)),
                  tok(bw(bq)), tok(bw(bk)), tok(bw(bv)), gsp(bw(0)),
                  pl.BlockSpec((1, LANES), lambda b, i: (0, 0)),
                  c_spec, v_spec, v_spec, c_spec, v_spec, v_spec],
        out_specs=[tok(fw(0)), tok(bw(0)), c_spec, v_spec, v_spec, c_spec, v_spec, v_spec],
        out_shape=[jax.ShapeDtypeStruct((B, T, BRANCH_W), F32)] * 2 + [c_shape, v_shape, v_shape] * 2,
        scratch_shapes=[pltpu.VMEM((H, dh, dh), F32), pltpu.VMEM((H, 1, LANES), F32), pltpu.VMEM((H, 1, LANES), F32)] * 2,
        compiler_params=_cparams("parallel", "arbitrary"),
        name="mlstm_scan",
    )(z, z, z, zg, z, z, z, zg, bias, *state)
    return outs[0], outs[1], tuple(outs[2:])


def _qkprep_body(rope, q_ref, k_ref, gq_ref, gk_ref, *rest):
    if rope:
        cos_ref, sa_ref, sb_ref, qo_ref, ko_ref = rest
    else:
        qo_ref, ko_ref = rest
    tm = q_ref.shape[1]
    r = lax.broadcasted_iota(jnp.int32, (LANES, LANES), 0) // HEAD_DIM
    c = lax.broadcasted_iota(jnp.int32, (LANES, LANES), 1) // HEAD_DIM
    head_ones = jnp.where(r == c, 1.0, 0.0).astype(BF16)
    lo = lax.broadcasted_iota(jnp.int32, (tm, LANES), 1) < HEAD_DIM

    def norm_rope(x, g):
        ss = jnp.dot((x * x).astype(BF16), head_ones, preferred_element_type=F32)
        y = x * lax.rsqrt(ss * (1.0 / HEAD_DIM) + EPS) * g
        if rope:
            y = y * cos_ref[...] + pltpu.roll(y, LANES - 16, 1) * sa_ref[...] + pltpu.roll(y, 16, 1) * sb_ref[...]
        return y

    for j in range(BRANCH_W // LANES):
        y = norm_rope(q_ref[0, :, j * LANES:(j + 1) * LANES].astype(F32), gq_ref[...]) * ATTN_SCALE
        even = jnp.where(lo, y, 0.0)
        odd = jnp.where(lo, 0.0, y)
        if j // 2 == 0:
            odd = pltpu.roll(odd, HEAD_DIM, 1)
        else:
            even = pltpu.roll(even, HEAD_DIM, 1)
        qo_ref[0, :, (2 * j) * LANES:(2 * j + 1) * LANES] = even.astype(BF16)
        qo_ref[0, :, (2 * j + 1) * LANES:(2 * j + 2) * LANES] = odd.astype(BF16)
    ko_ref[0] = norm_rope(k_ref[0].astype(F32), gk_ref[...]).astype(BF16)


def _qkprep(z, gq, gk, tables, tm):
    B, T, _ = z.shape
    rope = tables is not None
    in_specs = [pl.BlockSpec((1, tm, BRANCH_W), lambda b, i: (b, i, COL_AQ // BRANCH_W)),
                pl.BlockSpec((1, tm, LANES), lambda b, i: (b, i, COL_AK // LANES)),
                pl.BlockSpec((1, LANES), lambda b, i: (0, 0)),
                pl.BlockSpec((1, LANES), lambda b, i: (0, 0))]
    args = [z, z, gq, gk]
    if rope:
        in_specs += [pl.BlockSpec((tm, LANES), lambda b, i: (i, 0))] * 3
        args += list(tables)
    return pl.pallas_call(
        functools.partial(_qkprep_body, rope),
        grid=(B, T // tm),
        in_specs=in_specs,
        out_specs=[pl.BlockSpec((1, tm, 2 * BRANCH_W), lambda b, i: (b, i, 0)),
                   pl.BlockSpec((1, tm, LANES), lambda b, i: (b, i, 0))],
        out_shape=[jax.ShapeDtypeStruct((B, T, 2 * BRANCH_W), BF16), jax.ShapeDtypeStruct((B, T, LANES), BF16)],
        compiler_params=_cparams("parallel", "parallel"),
        name="qk_prep",
    )(*args)


def _rope_tables(n_tok):
    half = HEAD_DIM // 2
    t = jnp.arange(n_tok)
    inv = ROPE_THETA ** (-jnp.arange(0, half, 2, dtype=F32) / half)
    ang_r = (t // GRID_W).astype(F32)[:, None] * inv
    ang_c = (t % GRID_W).astype(F32)[:, None] * inv
    ang = jnp.concatenate([ang_r, ang_r, ang_c, ang_c], axis=-1)
    cos, sin = jnp.cos(ang), jnp.sin(ang)
    even_quarter = ((np.arange(HEAD_DIM) // (HEAD_DIM // 4)) % 2 == 0)[None, :]
    sa = jnp.where(even_quarter, -sin, 0.0)
    sb = jnp.where(even_quarter, 0.0, sin)
    tile = lambda a: jnp.concatenate([a, a], axis=-1)
    return tile(cos), tile(sa), tile(sb)


def _gqa_body(tk, q_ref, k_ref, v_ref, o_ref):
    g = pl.program_id(1)
    tq = q_ref.shape[1]
    S = k_ref.shape[1]
    n_heads = BRANCH_W // LANES
    q = jnp.concatenate([q_ref[0, :, h * LANES:(h + 1) * LANES] for h in range(n_heads)], axis=0)
    bounds = [0] + list(range(S % tk, S, tk)) if S % tk else list(range(0, S, tk))
    m = acc = None
    for off, end in zip(bounds, bounds[1:] + [S]):
        k = k_ref[0, off:end, :]
        v = v_ref[0, off:end, :]
        s = lax.dot_general(q, k, _NT, preferred_element_type=F32)
        mc = jnp.max(s, axis=-1, keepdims=True)
        if m is None:
            m = mc
            acc = jnp.dot(jnp.exp2(s - m).astype(BF16), v, preferred_element_type=F32)
        else:
            m_new = jnp.maximum(m, mc)
            acc = jnp.exp2(m - m_new) * acc + jnp.dot(jnp.exp2(s - m_new).astype(BF16), v, preferred_element_type=F32)
            m = m_new
    o = acc[:, :LANES] / acc[:, LANES:]
    lo = lax.broadcasted_iota(jnp.int32, (tq, LANES), 1) < HEAD_DIM
    first = g == 0
    for j in range(n_heads // 2):
        a = o[(2 * j) * tq:(2 * j + 1) * tq]
        b = o[(2 * j + 1) * tq:(2 * j + 2) * tq]
        pair0 = jnp.where(lo, a, pltpu.roll(b, HEAD_DIM, 1))
        pair1 = jnp.where(lo, pltpu.roll(a, HEAD_DIM, 1), b)
        o_ref[0, :, j * LANES:(j + 1) * LANES] = jnp.where(first, pair0, pair1).astype(BF16)


def _gqa(qp, k_all, v_all, tq, tk):
    B, T, _ = qp.shape
    S = k_all.shape[1]
    n_kv = 2
    return pl.pallas_call(
        functools.partial(_gqa_body, tk),
        grid=(B, n_kv, T // tq),
        in_specs=[pl.BlockSpec((1, tq, BRANCH_W), lambda b, g, i: (b, i, g)),
                  pl.BlockSpec((1, S, LANES), lambda b, g, i: (b, 0, 0)),
                  pl.BlockSpec((1, S, 2 * LANES), lambda b, g, i: (b, 0, 0))],
        out_specs=pl.BlockSpec((1, tq, BRANCH_W // n_kv), lambda b, g, i: (b, i, g)),
        out_shape=jax.ShapeDtypeStruct((B, T, BRANCH_W), BF16),
        compiler_params=_cparams("parallel", "parallel", "parallel"),
        name="gqa_flash",
    )(qp, k_all, v_all)


def _na_body(has_win, wk, rows, q_ref, *rest):
    if has_win:
        k_ref, v_ref, kc_ref, vc_ref, bm_ref, o_ref = rest
    else:
        kc_ref, vc_ref, o_ref = rest
    i = pl.program_id(1)
    tq = q_ref.shape[1]
    lo = lax.broadcasted_iota(jnp.int32, (tq, LANES), 1) < HEAD_DIM
    if has_win:
        ks = jnp.clip(i * NA_ROWS - NA_WIN_R // 2, 0, rows - wk // GRID_W)
        off = pl.multiple_of(ks * GRID_W, GRID_W)
    for j in range(BRANCH_W // LANES):
        sl = slice(j * LANES, (j + 1) * LANES)
        q2 = q_ref[0, :, sl]
        kc = kc_ref[0, :, sl]
        vc = jnp.concatenate([vc_ref[0, :, sl], jnp.ones((kc.shape[0], LANES), BF16)], axis=-1)
        if has_win:
            kw = k_ref[0, pl.ds(off, wk), sl]
            vw = jnp.concatenate([v_ref[0, pl.ds(off, wk), sl], jnp.ones((wk, LANES), BF16)], axis=-1)
        halves = []
        for e in range(2):
            qh = jnp.where(lo if e == 0 else jnp.logical_not(lo), q2, jnp.zeros_like(q2))
            sc = lax.dot_general(qh, kc, _NT, preferred_element_type=F32)
            m = jnp.max(sc, axis=-1, keepdims=True)
            if has_win:
                sw = lax.dot_general(qh, kw, _NT, preferred_element_type=F32) + bm_ref[0, 2 * j + e]
                m = jnp.maximum(m, jnp.max(sw, axis=-1, keepdims=True))
            acc = jnp.dot(jnp.exp2(sc - m).astype(BF16), vc, preferred_element_type=F32)
            if has_win:
                acc = acc + jnp.dot(jnp.exp2(sw - m).astype(BF16), vw, preferred_element_type=F32)
            halves.append(acc[:, :LANES] / acc[:, LANES:])
        o_ref[0, :, sl] = jnp.where(lo, halves[0], halves[1]).astype(BF16)


def _na(z, zc, bm):
    B, T, _ = z.shape
    Tc = zc.shape[1]
    rows = T // GRID_W
    tq = NA_ROWS * GRID_W
    wk = (NA_ROWS + NA_WIN_R - 1) * GRID_W
    nb = rows // NA_ROWS
    bq, bk, bv = COL_NQ // BRANCH_W, COL_NK // BRANCH_W, COL_NV // BRANCH_W

    def bm_index(b, i):
        return (jnp.where(i == 0, 0, jnp.where(i == nb - 1, 2, 1)), 0, 0, 0)

    return pl.pallas_call(
        functools.partial(_na_body, True, wk, rows),
        grid=(B, nb),
        in_specs=[pl.BlockSpec((1, tq, BRANCH_W), lambda b, i: (b, i, bq)),
                  pl.BlockSpec((1, T, BRANCH_W), lambda b, i: (b, 0, bk)),
                  pl.BlockSpec((1, T, BRANCH_W), lambda b, i: (b, 0, bv)),
                  pl.BlockSpec((1, Tc, BRANCH_W), lambda b, i: (b, 0, bk)),
                  pl.BlockSpec((1, Tc, BRANCH_W), lambda b, i: (b, 0, bv)),
                  pl.BlockSpec((1, 2 * BRANCH_W // LANES, tq, wk), bm_index)],
        out_specs=pl.BlockSpec((1, tq, BRANCH_W), lambda b, i: (b, i, 0)),
        out_shape=jax.ShapeDtypeStruct((B, T, BRANCH_W), BF16),
        compiler_params=_cparams("parallel", "arbitrary"),
        name="na_window",
    )(z, z, z, zc, zc, bm)


def _na_ctx(zc):
    B, Tc, _ = zc.shape
    bq, bk, bv = COL_NQ // BRANCH_W, COL_NK // BRANCH_W, COL_NV // BRANCH_W
    return pl.pallas_call(
        functools.partial(_na_body, False, 0, 0),
        grid=(B, 1),
        in_specs=[pl.BlockSpec((1, Tc, BRANCH_W), lambda b, i: (b, 0, bq)),
                  pl.BlockSpec((1, Tc, BRANCH_W), lambda b, i: (b, 0, bk)),
                  pl.BlockSpec((1, Tc, BRANCH_W), lambda b, i: (b, 0, bv))],
        out_specs=pl.BlockSpec((1, Tc, BRANCH_W), lambda b, i: (b, 0, 0)),
        out_shape=jax.ShapeDtypeStruct((B, Tc, BRANCH_W), BF16),
        compiler_params=_cparams("parallel", "arbitrary"),
        name="na_context",
    )(zc, zc, zc)


def _na_bias_tables(rpb, rows):
    R, W = NA_ROWS, GRID_W
    wr, wc = min(NA_WIN_R, rows), NA_WIN_C
    nkr = R + NA_WIN_R - 1
    nb = rows // R
    depth, heads = rpb.shape[:2]
    c = np.arange(W)[:, None]
    kc = np.arange(W)[None, :]
    c0 = np.clip(c - wc // 2, 0, W - wc)
    col_ok = (kc >= c0) & (kc < c0 + wc)
    onehot = ((kc - c + (NA_WIN_C - 1))[None] == np.arange(2 * NA_WIN_C - 1)[:, None, None]) & col_ok[None]
    toep = jnp.einsum('lhrd,dcx->lhrcx', rpb * LOG2E, jnp.asarray(onehot, F32), precision=lax.Precision.HIGHEST)
    toep = jnp.where(col_ok, toep, NEG)
    masked = jnp.full((depth, heads, W, W), NEG, F32)
    kinds = []
    for blk in (0, 1, nb - 1):
        rb = blk * R
        ks = int(np.clip(rb - NA_WIN_R // 2, 0, rows - nkr))
        q_rows = []
        for ri in range(R):
            r = rb + ri
            r0 = int(np.clip(r - wr // 2, 0, rows - wr))
            tiles = [toep[:, :, a - r + (NA_WIN_R - 1)] if r0 <= a < r0 + wr else masked for a in range(ks, ks + nkr)]
            q_rows.append(jnp.concatenate(tiles, axis=-1))
        kinds.append(jnp.concatenate(q_rows, axis=-2))
    return jnp.stack(kinds, axis=1)


def _merge_body(hf_ref, hb_ref, zo_ref, zg_ref, ga_ref, na_ref, x_ref, gml_ref, wb_ref, wo_ref, gpost_ref, mod_ref, o_ref):
    h = hf_ref[0] + hb_ref[0]
    hn = jnp.concatenate([_rms(h[:, t * ML_HEAD_DIM:(t + 1) * ML_HEAD_DIM]) for t in range(ML_HEADS)], axis=-1)
    y_ml = (hn * gml_ref[...] * _sigmoid(zo_ref[0].astype(F32))).astype(BF16)
    acc = None
    for n, y in enumerate((y_ml, ga_ref[0], na_ref[0])):
        gate = _sigmoid(zg_ref[0, :, n * D_MODEL:(n + 1) * D_MODEL].astype(F32))
        term = gate * jnp.dot(y, wb_ref[n], preferred_element_type=F32)
        acc = term if acc is None else acc + term
    o = jnp.dot(acc.astype(BF16), wo_ref[...], preferred_element_type=F32)
    o_ref[0] = x_ref[0] + mod_ref[0, 2:3, :] * (_rms(o) * gpost_ref[...])


def _merge(hf, hb, z, y_ga, y_na, x, g_ml, wb, wo, g_post, mod, tm):
    B, T, d = x.shape
    tok = lambda w, blk: pl.BlockSpec((1, tm, w), lambda b, i: (b, i, blk))
    full = lambda a: pl.BlockSpec(a.shape, lambda b, i: (0,) * a.ndim)
    g_ml = g_ml.reshape(1, BRANCH_W)
    g_post = g_post.reshape(1, d)
    return pl.pallas_call(
        _merge_body,
        grid=(B, T // tm),
        in_specs=[tok(BRANCH_W, 0), tok(BRANCH_W, 0), tok(BRANCH_W, COL_MO // BRANCH_W), tok(3 * d, 0),
                  tok(BRANCH_W, 0), tok(BRANCH_W, 0), tok(d, 0),
                  full(g_ml), full(wb), full(wo), full(g_post),
                  pl.BlockSpec((1, 6, d), _mod_index(mod))],
        out_specs=tok(d, 0),
        out_shape=jax.ShapeDtypeStruct((B, T, d), F32),
        compiler_params=_cparams("parallel", "parallel"),
        name="merge_out",
    )(hf, hb, z, z, y_ga, y_na, x, g_ml, wb, wo, g_post, mod)


def _ffn_body(x_ref, xp_ref, xn_ref, g_ref, mod_ref, wa_ref, wg_ref, cwa_ref, cwg_ref, cba_ref, cbg_ref, wd_ref, gpost_ref,
              o_ref, h_scr, acc_scr):
    i = pl.program_id(1)
    f = pl.program_id(2)
    tm = x_ref.shape[1]
    edge = xp_ref.shape[1]
    n_rows = tm + 2 * edge

    @pl.when(f == 0)
    def _():
        def nm(x):
            return _rms(x) * g_ref[...] * (1.0 + mod_ref[0, 4:5, :]) + mod_ref[0, 3:4, :]
        has_prev = (i > 0).astype(F32)
        has_next = (i < pl.num_programs(1) - 1).astype(F32)
        h_scr[0:edge] = (nm(xp_ref[0]) * has_prev).astype(BF16)
        h_scr[edge:edge + tm] = nm(x_ref[0]).astype(BF16)
        h_scr[edge + tm:n_rows] = (nm(xn_ref[0]) * has_next).astype(BF16)
        acc_scr[...] = jnp.zeros(acc_scr.shape, F32)

    h2 = h_scr[...]

    def conv_branch(w_ref, cw_ref, cb_ref):
        u = jnp.dot(h2, w_ref[...], preferred_element_type=F32)
        u_prev = pltpu.roll(u, 1, 0)[edge:edge + tm]
        u_next = pltpu.roll(u, n_rows - 1, 0)[edge:edge + tm]
        return u_prev * cw_ref[0:1, :] + u[edge:edge + tm] * cw_ref[1:2, :] + u_next * cw_ref[2:3, :] + cb_ref[...]

    a = conv_branch(wa_ref, cwa_ref, cba_ref)
    g = conv_branch(wg_ref, cwg_ref, cbg_ref)
    act = (a * (g * _sigmoid(g))).astype(BF16)
    acc_scr[...] += jnp.dot(act, wd_ref[...], preferred_element_type=F32)

    @pl.when(f == pl.num_programs(2) - 1)
    def _():
        o_ref[0] = x_ref[0] + mod_ref[0, 5:6, :] * (_rms(acc_scr[...]) * gpost_ref[...])


def _ffn(x, g_pre, mod, w_up, cw, cb, w_down, g_post, tm, tf):
    B, T, d = x.shape
    edge = 8
    nf = FFN_DIM // tf
    per = tm // edge
    last_edge = T // edge - 1
    return pl.pallas_call(
        _ffn_body,
        grid=(B, T // tm, nf),
        in_specs=[pl.BlockSpec((1, tm, d), lambda b, i, f: (b, i, 0)),
                  pl.BlockSpec((1, edge, d), lambda b, i, f: (b, jnp.maximum(i * per - 1, 0), 0)),
                  pl.BlockSpec((1, edge, d), lambda b, i, f: (b, jnp.minimum((i + 1) * per, last_edge), 0)),
                  pl.BlockSpec((1, d), lambda b, i, f: (0, 0)),
                  pl.BlockSpec((1, 6, d), _mod_index(mod)),
                  pl.BlockSpec((d, tf), lambda b, i, f: (0, f)),
                  pl.BlockSpec((d, tf), lambda b, i, f: (0, nf + f)),
                  pl.BlockSpec((3, tf), lambda b, i, f: (0, f)),
                  pl.BlockSpec((3, tf), lambda b, i, f: (0, nf + f)),
                  pl.BlockSpec((1, tf), lambda b, i, f: (0, f)),
                  pl.BlockSpec((1, tf), lambda b, i, f: (0, nf + f)),
                  pl.BlockSpec((tf, d), lambda b, i, f: (f, 0)),
                  pl.BlockSpec((1, d), lambda b, i, f: (0, 0))],
        out_specs=pl.BlockSpec((1, tm, d), lambda b, i, f: (b, i, 0)),
        out_shape=jax.ShapeDtypeStruct((B, T, d), F32),
        scratch_shapes=[pltpu.VMEM((tm + 2 * edge, d), BF16), pltpu.VMEM((tm, d), F32)],
        compiler_params=_cparams("parallel", "parallel", "arbitrary"),
        name="conv_ffn",
    )(x, x, x, g_pre.reshape(1, d), mod, w_up, w_up, cw, cw, cb.reshape(1, -1), cb.reshape(1, -1), w_down, g_post.reshape(1, d))


def _prep_w_in(w_in):
    sizes = (512, 512, 512, 512, 16, 512, 128, 128, 512, 512, 512, 3072)
    offs = np.concatenate([[0], np.cumsum(sizes)])
    seg = lambda n: w_in[:, :, offs[n]:offs[n + 1]]
    mq, mk, mv, mo, mg, aq, ak, av, nq, nk, nv, zg = (seg(n) for n in range(12))
    mk = mk * (ML_HEAD_DIM ** -0.5)
    nq = nq * ATTN_SCALE
    main = jnp.concatenate([zg, mq, mk, mv, mo, aq, nq, nk, nv, ak, av], axis=-1).astype(BF16)
    gate = jnp.pad(mg, ((0, 0), (0, 0), (0, LANES - mg.shape[-1]))).astype(BF16)
    return main, gate


def kernel(x, c, ctx, c_ctx, w_mod, b_mod, g_pre_mix, g_post_mix, g_pre_ffn, g_post_ffn, w_in, b_ml_gates, g_ml_out, g_q, g_k, na_rpb, w_branch, w_out, w_up, conv_w, conv_b, w_down):
    B, T, d = x.shape
    Tc = ctx.shape[1]
    depth = w_mod.shape[0]
    rows = T // GRID_W

    w_main, w_gate = _prep_w_in(w_in)
    wb, wo, wu, wd = (a.astype(BF16) for a in (w_branch, w_out, w_up, w_down))
    n_mod = -(-(B + 1) // 8) * 8
    cc = jnp.concatenate([c, c_ctx[None], jnp.zeros((n_mod - B - 1, d), F32)], axis=0)
    mod_all = _mod_call(cc, w_mod, b_mod)
    tables = _rope_tables(T)
    bias_tables = _na_bias_tables(na_rpb, rows)
    gate_bias = jnp.pad(b_ml_gates, ((0, 0), (0, LANES - b_ml_gates.shape[-1])))
    gq2 = jnp.concatenate([g_q, g_q], axis=-1)
    gk2 = jnp.concatenate([g_k, g_k], axis=-1)
    zero_state = _zero_state(B)

    tm = min(TOKEN_TILE, T)
    tq = min(GQA_TQ, T)
    xc = ctx
    for l in range(depth):
        last = l == depth - 1
        mod_l = mod_all[l, :B].reshape(B, 6, d)
        mod_c = mod_all[l, B:B + 1].reshape(1, 6, d)
        bias_l = gate_bias[l][None]

        z, zg = _inproj(x, g_pre_mix[l], mod_l, w_main[l], w_gate[l], tm)
        zc, zgc = _inproj(xc, g_pre_mix[l], mod_c, w_main[l], w_gate[l], Tc)

        hfc, hbc, state = _mlstm(zc, zgc, bias_l, zero_state)
        hf, hb, _ = _mlstm(z, zg, bias_l, state)

        qp, kn = _qkprep(z, gq2[l][None], gk2[l][None], tables, tm)
        qpc, knc = _qkprep(zc, gq2[l][None], gk2[l][None], None, Tc)
        vc_a = jnp.concatenate([zc[:, :, COL_AV:COL_AV + LANES], jnp.ones((B, Tc, LANES), BF16)], axis=-1)
        v_a = jnp.concatenate([z[:, :, COL_AV:COL_AV + LANES], jnp.ones((B, T, LANES), BF16)], axis=-1)
        k_all = jnp.concatenate([knc, kn], axis=1)
        v_all = jnp.concatenate([vc_a, v_a], axis=1)
        y_ga = _gqa(qp, k_all, v_all, tq, GQA_KEY_CHUNK)
        y_na = _na(z, zc, bias_tables[l])

        x_mid = _merge(hf, hb, z, y_ga, y_na, x, g_ml_out[l], wb[l], wo[l], g_post_mix[l], mod_l, tm)
        x_new = _ffn(x_mid, g_pre_ffn[l], mod_l, wu[l], conv_w[l], conv_b[l], wd[l], g_post_ffn[l], min(FFN_TM, T), FFN_TF)

        if not last:
            yc_ga = _gqa(qpc, knc, vc_a, min(128, Tc), Tc)
            yc_na = _na_ctx(zc)
            xc_mid = _merge(hfc, hbc, zc, yc_ga, yc_na, xc, g_ml_out[l], wb[l], wo[l], g_post_mix[l], mod_c, Tc)
            xc = _ffn(xc_mid, g_pre_ffn[l], mod_c, wu[l], conv_w[l], conv_b[l], wd[l], g_post_ffn[l], Tc, FFN_TF)
        x = x_new
    return x
```

```python
import functools

import numpy as np
import jax
import jax.numpy as jnp
from jax import lax
from jax.experimental import pallas as pl
from jax.experimental.pallas import tpu as pltpu

F32 = jnp.float32
BF16 = jnp.bfloat16

D_MODEL = 1024
BRANCH_W = 512
HEAD_DIM = 64
ML_HEADS = 4
ML_HEAD_DIM = 128
GRID_W = 64
NA_WIN_R = 8
NA_WIN_C = 16
ROPE_THETA = 10000.0
FFN_DIM = 2816
EPS = 1e-6
LANES = 128
ML_CHUNK = 256
NA_ROWS = 4
GQA_KEY_CHUNK = 256
GQA_TQ = 512
TOKEN_TILE = 512
FFN_TM = 512
FFN_TF = FFN_DIM
NEG = -1e30
LOG2E = 1.4426950408889634
ATTN_SCALE = HEAD_DIM ** -0.5 * LOG2E
VMEM_LIMIT = 56 * 1024 * 1024

NZ = 7424
COL_ZG = 0
COL_MQ, COL_MK, COL_MV, COL_MO = 3072, 3584, 4096, 4608
COL_AQ = 5120
COL_NQ, COL_NK, COL_NV = 5632, 6144, 6656
COL_AK, COL_AV = 7168, 7296

_NT = (((1,), (1,)), ((), ()))
_TN = (((0,), (0,)), ((), ()))


def _cparams(*sem):
    return pltpu.CompilerParams(dimension_semantics=sem, vmem_limit_bytes=VMEM_LIMIT)


def _rms(x):
    return x * lax.rsqrt(jnp.mean(x * x, axis=-1, keepdims=True) + EPS)


def _sigmoid(x):
    return 1.0 / (1.0 + jnp.exp(-x))


def _mod_body(c_ref, w_ref, b_ref, o_ref):
    c = c_ref[...]
    s = (c * _sigmoid(c)).astype(BF16)
    o_ref[0] = jnp.dot(s, w_ref[0].astype(BF16), preferred_element_type=F32) + b_ref[0]


def _mod_call(cc, w_mod, b_mod):
    depth, d, n = w_mod.shape
    tn = 1536
    return pl.pallas_call(
        _mod_body,
        grid=(depth, n // tn),
        in_specs=[pl.BlockSpec((cc.shape[0], d), lambda l, j: (0, 0)),
                  pl.BlockSpec((1, d, tn), lambda l, j: (l, 0, j)),
                  pl.BlockSpec((1, 1, tn), lambda l, j: (l, 0, j))],
        out_specs=pl.BlockSpec((1, cc.shape[0], tn), lambda l, j: (l, 0, j)),
        out_shape=jax.ShapeDtypeStruct((depth, cc.shape[0], n), F32),
        compiler_params=_cparams("parallel", "parallel"),
        name="modulation",
    )(cc, w_mod, b_mod.reshape(depth, 1, n))


def _inproj_body(x_ref, g_ref, mod_ref, w_ref, wg_ref, z_ref, zg_ref, h_scr):
    @pl.when(pl.program_id(2) == 0)
    def _():
        y = _rms(x_ref[0]) * g_ref[...]
        h = (y * (1.0 + mod_ref[0, 1:2, :]) + mod_ref[0, 0:1, :]).astype(BF16)
        h_scr[...] = h
        zg_ref[0] = jnp.dot(h, wg_ref[...], preferred_element_type=F32)

    z_ref[0] = jnp.dot(h_scr[...], w_ref[...], preferred_element_type=F32).astype(BF16)


def _mod_index(mod):
    if mod.shape[0] > 1:
        return lambda b, *_: (b, 0, 0)
    return lambda b, *_: (0, 0, 0)


def _inproj(x, g, mod, w, wg, tm):
    B, T, d = x.shape
    tn = NZ // 2
    return pl.pallas_call(
        _inproj_body,
        grid=(B, T // tm, NZ // tn),
        in_specs=[pl.BlockSpec((1, tm, d), lambda b, i, j: (b, i, 0)),
                  pl.BlockSpec((1, d), lambda b, i, j: (0, 0)),
                  pl.BlockSpec((1, 6, d), _mod_index(mod)),
                  pl.BlockSpec((d, tn), lambda b, i, j: (0, j)),
                  pl.BlockSpec((d, LANES), lambda b, i, j: (0, 0))],
        out_specs=[pl.BlockSpec((1, tm, tn), lambda b, i, j: (b, i, j)),
                   pl.BlockSpec((1, tm, LANES), lambda b, i, j: (b, i, 0))],
        out_shape=[jax.ShapeDtypeStruct((B, T, NZ), BF16), jax.ShapeDtypeStruct((B, T, LANES), F32)],
        scratch_shapes=[pltpu.VMEM((tm, d), BF16)],
        compiler_params=_cparams("parallel", "parallel", "arbitrary"),
        name="inproj",
    )(x, g.reshape(1, d), mod, w, wg)


def _log_sigmoid(x):
    return jnp.minimum(x, 0.0) - jnp.log1p(jnp.exp(-jnp.abs(x)))


def _split_bf16(a, parts):
    out = []
    r = a
    for _ in range(parts):
        p = r.astype(BF16)
        out.append(p)
        r = r - p.astype(F32)
    return out


def _twice(a):
    return jnp.concatenate([a, a], axis=-1)


def _mlstm_chunk(q, k, v1, b_rep, ig_rep, b_row, ig_row, b_last, mask, c_ext, m):
    s0 = lax.dot_general(q, k, _NT, preferred_element_type=F32)
    logd = jnp.where(mask, _twice(b_rep) - (b_row - ig_row), NEG)
    bm = b_rep + m
    m_t = jnp.maximum(bm, jnp.max(logd, axis=-1, keepdims=True))
    s = s0 * jnp.exp(logd - _twice(m_t))
    inter = jnp.exp(bm - m_t)
    q_c = jnp.dot(q, c_ext.astype(BF16), preferred_element_type=F32)
    tot = jnp.dot(s.astype(BF16), v1, preferred_element_type=F32) + _twice(inter) * q_c
    h = tot[:, :ML_HEAD_DIM] / jnp.maximum(jnp.abs(tot[:, ML_HEAD_DIM:]), jnp.exp(-m_t))

    lw = b_last - b_rep + ig_rep
    m_new = jnp.maximum(b_last + m, jnp.max(lw, axis=0, keepdims=True))
    w = jnp.exp(lw - m_new)
    decay = jnp.exp(b_last + m - m_new)
    wv = (_twice(w) * v1.astype(F32)).astype(BF16)
    c_new = _twice(decay) * c_ext + lax.dot_general(k, wv, _TN, preferred_element_type=F32)
    return h, c_new, m_new


def _mlstm_body(qf_ref, kf_ref, vf_ref, gf_ref, qb_ref, kb_ref, vb_ref, gb_ref, bias_ref,
                c0f_ref, m0f_ref, c0b_ref, m0b_ref,
                hf_ref, hb_ref, c1f_ref, m1f_ref, c1b_ref, m1b_ref,
                cf_scr, mf_scr, cb_scr, mb_scr):
    i = pl.program_id(1)
    L = ML_CHUNK
    H, dh = ML_HEADS, ML_HEAD_DIM

    @pl.when(i == 0)
    def _():
        cf_scr[...] = c0f_ref[0]
        mf_scr[...] = m0f_ref[0]
        cb_scr[...] = c0b_ref[0]
        mb_scr[...] = m0b_ref[0]

    row = lax.broadcasted_iota(jnp.int32, (L, L), 0)
    col = lax.broadcasted_iota(jnp.int32, (L, L), 1)
    lane = lax.broadcasted_iota(jnp.int32, (L, LANES), 1)
    is_forget = (lane % 8) >= 4
    eye = jnp.where(row == col, 1.0, 0.0).astype(BF16)
    ones = jnp.ones((L, dh), BF16)

    def replicate(mat, rows):
        rhs = jnp.concatenate([jnp.broadcast_to(r, (LANES, L)) for r in rows], axis=0)
        return sum(lax.dot_general(mat, p, _NT, preferred_element_type=F32) for p in _split_bf16(rhs, 2))

    def run(fwd, q_ref, k_ref, v_ref, g_ref, h_ref, c_scr, m_scr):
        mask = (row >= col) if fwd else (row <= col)
        tri = jnp.where(mask, 1.0, 0.0).astype(BF16)
        g = g_ref[0] + bias_ref[...]
        G = jnp.where(is_forget, _log_sigmoid(g), g)
        r0 = 0 if fwd else 2 * H
        R = G.T[r0:r0 + 2 * H]
        Bsum = sum(lax.dot_general(p, tri, _NT, preferred_element_type=F32) for p in _split_bf16(R, 3))
        for pair in range(H // 2):
            heads = (2 * pair, 2 * pair + 1)
            b_rep2 = replicate(tri, [R[H + hh:H + hh + 1] for hh in heads])
            ig_rep2 = replicate(eye, [R[hh:hh + 1] for hh in heads])
            for j, hh in enumerate(heads):
                b_rep = b_rep2[:, j * LANES:(j + 1) * LANES]
                ig_rep = ig_rep2[:, j * LANES:(j + 1) * LANES]
                b_last = b_rep[L - 1:L, :] if fwd else b_rep[0:1, :]
                sl = slice(hh * dh, (hh + 1) * dh)
                v1 = jnp.concatenate([v_ref[0, :, sl], ones], axis=-1)
                h, c_new, m_new = _mlstm_chunk(
                    q_ref[0, :, sl], k_ref[0, :, sl], v1, b_rep, ig_rep,
                    Bsum[H + hh:H + hh + 1], R[hh:hh + 1], b_last, mask, c_scr[hh], m_scr[hh])
                h_ref[0, :, sl] = h
                c_scr[hh] = c_new
                m_scr[hh] = m_new

    run(True, qf_ref, kf_ref, vf_ref, gf_ref, hf_ref, cf_scr, mf_scr)
    run(False, qb_ref, kb_ref, vb_ref, gb_ref, hb_ref, cb_scr, mb_scr)

    @pl.when(i == pl.num_programs(1) - 1)
    def _():
        c1f_ref[0] = cf_scr[...]
        m1f_ref[0] = mf_scr[...]
        c1b_ref[0] = cb_scr[...]
        m1b_ref[0] = mb_scr[...]


def _zero_state(B):
    H, dh = ML_HEADS, ML_HEAD_DIM
    return (jnp.zeros((B, H, dh, 2 * dh), F32), jnp.zeros((B, H, 1, LANES), F32)) * 2


def _mlstm(z, zg, bias, state):
    B, T, _ = z.shape
    L = ML_CHUNK
    nc = T // L
    H, dh = ML_HEADS, ML_HEAD_DIM
    bq, bk, bv = COL_MQ // BRANCH_W, COL_MK // BRANCH_W, COL_MV // BRANCH_W
    fw = lambda blk: (lambda b, i: (b, i, blk))
    bw = lambda blk: (lambda b, i: (b, nc - 1 - i, blk))
    st4 = lambda b, i: (b, 0, 0, 0)
    c_spec = pl.BlockSpec((1, H, dh, 2 * dh), st4)
    m_spec = pl.BlockSpec((1, H, 1, LANES), st4)
    c_shape = jax.ShapeDtypeStruct((B, H, dh, 2 * dh), F32)
    m_shape = jax.ShapeDtypeStruct((B, H, 1, LANES), F32)
    tok = lambda idx: pl.BlockSpec((1, L, BRANCH_W), idx)
    gsp = lambda idx: pl.BlockSpec((1, L, LANES), idx)
    outs = pl.pallas_call(
        _mlstm_body,
        grid=(B, nc),
        in_specs=[tok(fw(bq)), tok(fw(bk)), tok(fw(bv)), gsp(fw(0)),
                  tok(bw(bq)), tok(bw(bk)), tok(bw(bv)), gsp(bw(0)),
                  pl.BlockSpec((1, LANES), lambda b, i: (0, 0)),
                  c_spec, m_spec, c_spec, m_spec],
        out_specs=[tok(fw(0)), tok(bw(0)), c_spec, m_spec, c_spec, m_spec],
        out_shape=[jax.ShapeDtypeStruct((B, T, BRANCH_W), F32)] * 2 + [c_shape, m_shape] * 2,
        scratch_shapes=[pltpu.VMEM((H, dh, 2 * dh), F32), pltpu.VMEM((H, 1, LANES), F32)] * 2,
        compiler_params=_cparams("parallel", "arbitrary"),
        name="mlstm_scan",
    )(z, z, z, zg, z, z, z, zg, bias, *state)
    return outs[0], outs[1], tuple(outs[2:])


def _qkprep_body(rope, q_ref, k_ref, gq_ref, gk_ref, *rest):
    if rope:
        cos_ref, sa_ref, sb_ref, qo_ref, ko_ref = rest
    else:
        qo_ref, ko_ref = rest
    tm = q_ref.shape[1]
    r = lax.broadcasted_iota(jnp.int32, (LANES, LANES), 0) // HEAD_DIM
    c = lax.broadcasted_iota(jnp.int32, (LANES, LANES), 1) // HEAD_DIM
    head_ones = jnp.where(r == c, 1.0, 0.0).astype(BF16)
    lo = lax.broadcasted_iota(jnp.int32, (tm, LANES), 1) < HEAD_DIM

    def norm_rope(x, g):
        ss = jnp.dot((x * x).astype(BF16), head_ones, preferred_element_type=F32)
        y = x * lax.rsqrt(ss * (1.0 / HEAD_DIM) + EPS) * g
        if rope:
            y = y * cos_ref[...] + pltpu.roll(y, LANES - 16, 1) * sa_ref[...] + pltpu.roll(y, 16, 1) * sb_ref[...]
        return y

    for j in range(BRANCH_W // LANES):
        y = norm_rope(q_ref[0, :, j * LANES:(j + 1) * LANES].astype(F32), gq_ref[...]) * ATTN_SCALE
        even = jnp.where(lo, y, 0.0)
        odd = jnp.where(lo, 0.0, y)
        if j // 2 == 0:
            odd = pltpu.roll(odd, HEAD_DIM, 1)
        else:
            even = pltpu.roll(even, HEAD_DIM, 1)
        qo_ref[0, :, (2 * j) * LANES:(2 * j + 1) * LANES] = even.astype(BF16)
        qo_ref[0, :, (2 * j + 1) * LANES:(2 * j + 2) * LANES] = odd.astype(BF16)
    ko_ref[0] = norm_rope(k_ref[0].astype(F32), gk_ref[...]).astype(BF16)


def _qkprep(z, gq, gk, tables, tm):
    B, T, _ = z.shape
    rope = tables is not None
    in_specs = [pl.BlockSpec((1, tm, BRANCH_W), lambda b, i: (b, i, COL_AQ // BRANCH_W)),
                pl.BlockSpec((1, tm, LANES), lambda b, i: (b, i, COL_AK // LANES)),
                pl.BlockSpec((1, LANES), lambda b, i: (0, 0)),
                pl.BlockSpec((1, LANES), lambda b, i: (0, 0))]
    args = [z, z, gq, gk]
    if rope:
        in_specs += [pl.BlockSpec((tm, LANES), lambda b, i: (i, 0))] * 3
        args += list(tables)
    return pl.pallas_call(
        functools.partial(_qkprep_body, rope),
        grid=(B, T // tm),
        in_specs=in_specs,
        out_specs=[pl.BlockSpec((1, tm, 2 * BRANCH_W), lambda b, i: (b, i, 0)),
                   pl.BlockSpec((1, tm, LANES), lambda b, i: (b, i, 0))],
        out_shape=[jax.ShapeDtypeStruct((B, T, 2 * BRANCH_W), BF16), jax.ShapeDtypeStruct((B, T, LANES), BF16)],
        compiler_params=_cparams("parallel", "parallel"),
        name="qk_prep",
    )(*args)


def _rope_tables(n_tok):
    half = HEAD_DIM // 2
    t = jnp.arange(n_tok)
    inv = ROPE_THETA ** (-jnp.arange(0, half, 2, dtype=F32) / half)
    ang_r = (t // GRID_W).astype(F32)[:, None] * inv
    ang_c = (t % GRID_W).astype(F32)[:, None] * inv
    ang = jnp.concatenate([ang_r, ang_r, ang_c, ang_c], axis=-1)
    cos, sin = jnp.cos(ang), jnp.sin(ang)
    even_quarter = ((np.arange(HEAD_DIM) // (HEAD_DIM // 4)) % 2 == 0)[None, :]
    sa = jnp.where(even_quarter, -sin, 0.0)
    sb = jnp.where(even_quarter, 0.0, sin)
    tile = lambda a: jnp.concatenate([a, a], axis=-1)
    return tile(cos), tile(sa), tile(sb)


def _gqa_body(tk, q_ref, k_ref, v_ref, o_ref):
    g = pl.program_id(1)
    tq = q_ref.shape[1]
    S = k_ref.shape[1]
    n_heads = BRANCH_W // LANES
    q = jnp.concatenate([q_ref[0, :, h * LANES:(h + 1) * LANES] for h in range(n_heads)], axis=0)
    bounds = [0] + list(range(S % tk, S, tk)) if S % tk else list(range(0, S, tk))
    m = acc = None
    for off, end in zip(bounds, bounds[1:] + [S]):
        k = k_ref[0, off:end, :]
        v = v_ref[0, off:end, :]
        s = lax.dot_general(q, k, _NT, preferred_element_type=F32)
        mc = jnp.max(s, axis=-1, keepdims=True)
        if m is None:
            m = mc
            acc = jnp.dot(jnp.exp2(s - m).astype(BF16), v, preferred_element_type=F32)
        else:
            m_new = jnp.maximum(m, mc)
            acc = jnp.exp2(m - m_new) * acc + jnp.dot(jnp.exp2(s - m_new).astype(BF16), v, preferred_element_type=F32)
            m = m_new
    o = acc[:, :LANES] / acc[:, LANES:]
    lo = lax.broadcasted_iota(jnp.int32, (tq, LANES), 1) < HEAD_DIM
    first = g == 0
    for j in range(n_heads // 2):
        a = o[(2 * j) * tq:(2 * j + 1) * tq]
        b = o[(2 * j + 1) * tq:(2 * j + 2) * tq]
        pair0 = jnp.where(lo, a, pltpu.roll(b, HEAD_DIM, 1))
        pair1 = jnp.where(lo, pltpu.roll(a, HEAD_DIM, 1), b)
        o_ref[0, :, j * LANES:(j + 1) * LANES] = jnp.where(first, pair0, pair1).astype(BF16)


def _gqa(qp, k_all, v_all, tq, tk):
    B, T, _ = qp.shape
    S = k_all.shape[1]
    n_kv = 2
    return pl.pallas_call(
        functools.partial(_gqa_body, tk),
        grid=(B, n_kv, T // tq),
        in_specs=[pl.BlockSpec((1, tq, BRANCH_W), lambda b, g, i: (b, i, g)),
                  pl.BlockSpec((1, S, LANES), lambda b, g, i: (b, 0, 0)),
                  pl.BlockSpec((1, S, 2 * LANES), lambda b, g, i: (b, 0, 0))],
        out_specs=pl.BlockSpec((1, tq, BRANCH_W // n_kv), lambda b, g, i: (b, i, g)),
        out_shape=jax.ShapeDtypeStruct((B, T, BRANCH_W), BF16),
        compiler_params=_cparams("parallel", "parallel", "parallel"),
        name="gqa_flash",
    )(qp, k_all, v_all)


def _na_body(has_win, wk, rows, q_ref, *rest):
    if has_win:
        k_ref, v_ref, kc_ref, vc_ref, bm_ref, o_ref = rest
    else:
        kc_ref, vc_ref, o_ref = rest
    i = pl.program_id(1)
    tq = q_ref.shape[1]
    lo = lax.broadcasted_iota(jnp.int32, (tq, LANES), 1) < HEAD_DIM
    if has_win:
        ks = jnp.clip(i * NA_ROWS - NA_WIN_R // 2, 0, rows - wk // GRID_W)
        off = pl.multiple_of(ks * GRID_W, GRID_W)
    for j in range(BRANCH_W // LANES):
        sl = slice(j * LANES, (j + 1) * LANES)
        q2 = q_ref[0, :, sl]
        kc = kc_ref[0, :, sl]
        vc = jnp.concatenate([vc_ref[0, :, sl], jnp.ones((kc.shape[0], LANES), BF16)], axis=-1)
        if has_win:
            kw = k_ref[0, pl.ds(off, wk), sl]
            vw = jnp.concatenate([v_ref[0, pl.ds(off, wk), sl], jnp.ones((wk, LANES), BF16)], axis=-1)
        halves = []
        for e in range(2):
            qh = jnp.where(lo if e == 0 else jnp.logical_not(lo), q2, jnp.zeros_like(q2))
            sc = lax.dot_general(qh, kc, _NT, preferred_element_type=F32)
            m = jnp.max(sc, axis=-1, keepdims=True)
            if has_win:
                sw = lax.dot_general(qh, kw, _NT, preferred_element_type=F32) + bm_ref[0, 2 * j + e]
                m = jnp.maximum(m, jnp.max(sw, axis=-1, keepdims=True))
            acc = jnp.dot(jnp.exp2(sc - m).astype(BF16), vc, preferred_element_type=F32)
            if has_win:
                acc = acc + jnp.dot(jnp.exp2(sw - m).astype(BF16), vw, preferred_element_type=F32)
            halves.append(acc[:, :LANES] / acc[:, LANES:])
        o_ref[0, :, sl] = jnp.where(lo, halves[0], halves[1]).astype(BF16)


def _na(z, zc, bm):
    B, T, _ = z.shape
    Tc = zc.shape[1]
    rows = T // GRID_W
    tq = NA_ROWS * GRID_W
    wk = (NA_ROWS + NA_WIN_R - 1) * GRID_W
    nb = rows // NA_ROWS
    bq, bk, bv = COL_NQ // BRANCH_W, COL_NK // BRANCH_W, COL_NV // BRANCH_W

    def bm_index(b, i):
        return (jnp.where(i == 0, 0, jnp.where(i == nb - 1, 2, 1)), 0, 0, 0)

    return pl.pallas_call(
        functools.partial(_na_body, True, wk, rows),
        grid=(B, nb),
        in_specs=[pl.BlockSpec((1, tq, BRANCH_W), lambda b, i: (b, i, bq)),
                  pl.BlockSpec((1, T, BRANCH_W), lambda b, i: (b, 0, bk)),
                  pl.BlockSpec((1, T, BRANCH_W), lambda b, i: (b, 0, bv)),
                  pl.BlockSpec((1, Tc, BRANCH_W), lambda b, i: (b, 0, bk)),
                  pl.BlockSpec((1, Tc, BRANCH_W), lambda b, i: (b, 0, bv)),
                  pl.BlockSpec((1, 2 * BRANCH_W // LANES, tq, wk), bm_index)],
        out_specs=pl.BlockSpec((1, tq, BRANCH_W), lambda b, i: (b, i, 0)),
        out_shape=jax.ShapeDtypeStruct((B, T, BRANCH_W), BF16),
        compiler_params=_cparams("parallel", "arbitrary"),
        name="na_window",
    )(z, z, z, zc, zc, bm)


def _na_ctx(zc):
    B, Tc, _ = zc.shape
    bq, bk, bv = COL_NQ // BRANCH_W, COL_NK // BRANCH_W, COL_NV // BRANCH_W
    return pl.pallas_call(
        functools.partial(_na_body, False, 0, 0),
        grid=(B, 1),
        in_specs=[pl.BlockSpec((1, Tc, BRANCH_W), lambda b, i: (b, 0, bq)),
                  pl.BlockSpec((1, Tc, BRANCH_W), lambda b, i: (b, 0, bk)),
                  pl.BlockSpec((1, Tc, BRANCH_W), lambda b, i: (b, 0, bv))],
        out_specs=pl.BlockSpec((1, Tc, BRANCH_W), lambda b, i: (b, 0, 0)),
        out_shape=jax.ShapeDtypeStruct((B, Tc, BRANCH_W), BF16),
        compiler_params=_cparams("parallel", "arbitrary"),
        name="na_context",
    )(zc, zc, zc)


def _na_bias_tables(rpb, rows):
    R, W = NA_ROWS, GRID_W
    wr, wc = min(NA_WIN_R, rows), NA_WIN_C
    nkr = R + NA_WIN_R - 1
    nb = rows // R
    depth, heads = rpb.shape[:2]
    c = np.arange(W)[:, None]
    kc = np.arange(W)[None, :]
    c0 = np.clip(c - wc // 2, 0, W - wc)
    col_ok = (kc >= c0) & (kc < c0 + wc)
    onehot = ((kc - c + (NA_WIN_C - 1))[None] == np.arange(2 * NA_WIN_C - 1)[:, None, None]) & col_ok[None]
    toep = jnp.einsum('lhrd,dcx->lhrcx', rpb * LOG2E, jnp.asarray(onehot, F32), precision=lax.Precision.HIGHEST)
    toep = jnp.where(col_ok, toep, NEG)
    masked = jnp.full((depth, heads, W, W), NEG, F32)
    kinds = []
    for blk in (0, 1, nb - 1):
        rb = blk * R
        ks = int(np.clip(rb - NA_WIN_R // 2, 0, rows - nkr))
        q_rows = []
        for ri in range(R):
            r = rb + ri
            r0 = int(np.clip(r - wr // 2, 0, rows - wr))
            tiles = [toep[:, :, a - r + (NA_WIN_R - 1)] if r0 <= a < r0 + wr else masked for a in range(ks, ks + nkr)]
            q_rows.append(jnp.concatenate(tiles, axis=-1))
        kinds.append(jnp.concatenate(q_rows, axis=-2))
    return jnp.stack(kinds, axis=1)


def _merge_body(hf_ref, hb_ref, zo_ref, zg_ref, ga_ref, na_ref, x_ref, gml_ref, wb_ref, wo_ref, gpost_ref, mod_ref, o_ref):
    h = hf_ref[0] + hb_ref[0]
    hn = jnp.concatenate([_rms(h[:, t * ML_HEAD_DIM:(t + 1) * ML_HEAD_DIM]) for t in range(ML_HEADS)], axis=-1)
    y_ml = (hn * gml_ref[...] * _sigmoid(zo_ref[0].astype(F32))).astype(BF16)
    acc = None
    for n, y in enumerate((y_ml, ga_ref[0], na_ref[0])):
        gate = _sigmoid(zg_ref[0, :, n * D_MODEL:(n + 1) * D_MODEL].astype(F32))
        term = gate * jnp.dot(y, wb_ref[n], preferred_element_type=F32)
        acc = term if acc is None else acc + term
    o = jnp.dot(acc.astype(BF16), wo_ref[...], preferred_element_type=F32)
    o_ref[0] = x_ref[0] + mod_ref[0, 2:3, :] * (_rms(o) * gpost_ref[...])


def _merge(hf, hb, z, y_ga, y_na, x, g_ml, wb, wo, g_post, mod, tm):
    B, T, d = x.shape
    tok = lambda w, blk: pl.BlockSpec((1, tm, w), lambda b, i: (b, i, blk))
    full = lambda a: pl.BlockSpec(a.shape, lambda b, i: (0,) * a.ndim)
    g_ml = g_ml.reshape(1, BRANCH_W)
    g_post = g_post.reshape(1, d)
    return pl.pallas_call(
        _merge_body,
        grid=(B, T // tm),
        in_specs=[tok(BRANCH_W, 0), tok(BRANCH_W, 0), tok(BRANCH_W, COL_MO // BRANCH_W), tok(3 * d, 0),
                  tok(BRANCH_W, 0), tok(BRANCH_W, 0), tok(d, 0),
                  full(g_ml), full(wb), full(wo), full(g_post),
                  pl.BlockSpec((1, 6, d), _mod_index(mod))],
        out_specs=tok(d, 0),
        out_shape=jax.ShapeDtypeStruct((B, T, d), F32),
        compiler_params=_cparams("parallel", "parallel"),
        name="merge_out",
    )(hf, hb, z, z, y_ga, y_na, x, g_ml, wb, wo, g_post, mod)


def _ffn_body(x_ref, xp_ref, xn_ref, g_ref, mod_ref, wa_ref, wg_ref, cwa_ref, cwg_ref, cba_ref, cbg_ref, wd_ref, gpost_ref,
              o_ref, h_scr, acc_scr):
    i = pl.program_id(1)
    f = pl.program_id(2)
    tm = x_ref.shape[1]
    edge = xp_ref.shape[1]
    n_rows = tm + 2 * edge

    @pl.when(f == 0)
    def _():
        def nm(x):
            return _rms(x) * g_ref[...] * (1.0 + mod_ref[0, 4:5, :]) + mod_ref[0, 3:4, :]
        has_prev = (i > 0).astype(F32)
        has_next = (i < pl.num_programs(1) - 1).astype(F32)
        h_scr[0:edge] = (nm(xp_ref[0]) * has_prev).astype(BF16)
        h_scr[edge:edge + tm] = nm(x_ref[0]).astype(BF16)
        h_scr[edge + tm:n_rows] = (nm(xn_ref[0]) * has_next).astype(BF16)
        acc_scr[...] = jnp.zeros(acc_scr.shape, F32)

    h2 = h_scr[...]

    def conv_branch(w_ref, cw_ref, cb_ref):
        u = jnp.dot(h2, w_ref[...], preferred_element_type=F32)
        u_prev = pltpu.roll(u, 1, 0)[edge:edge + tm]
        u_next = pltpu.roll(u, n_rows - 1, 0)[edge:edge + tm]
        return u_prev * cw_ref[0:1, :] + u[edge:edge + tm] * cw_ref[1:2, :] + u_next * cw_ref[2:3, :] + cb_ref[...]

    a = conv_branch(wa_ref, cwa_ref, cba_ref)
    g = conv_branch(wg_ref, cwg_ref, cbg_ref)
    act = (a * (g * _sigmoid(g))).astype(BF16)
    acc_scr[...] += jnp.dot(act, wd_ref[...], preferred_element_type=F32)

    @pl.when(f == pl.num_programs(2) - 1)
    def _():
        o_ref[0] = x_ref[0] + mod_ref[0, 5:6, :] * (_rms(acc_scr[...]) * gpost_ref[...])


def _ffn(x, g_pre, mod, w_up, cw, cb, w_down, g_post, tm, tf):
    B, T, d = x.shape
    edge = 8
    nf = FFN_DIM // tf
    per = tm // edge
    last_edge = T // edge - 1
    return pl.pallas_call(
        _ffn_body,
        grid=(B, T // tm, nf),
        in_specs=[pl.BlockSpec((1, tm, d), lambda b, i, f: (b, i, 0)),
                  pl.BlockSpec((1, edge, d), lambda b, i, f: (b, jnp.maximum(i * per - 1, 0), 0)),
                  pl.BlockSpec((1, edge, d), lambda b, i, f: (b, jnp.minimum((i + 1) * per, last_edge), 0)),
                  pl.BlockSpec((1, d), lambda b, i, f: (0, 0)),
                  pl.BlockSpec((1, 6, d), _mod_index(mod)),
                  pl.BlockSpec((d, tf), lambda b, i, f: (0, f)),
                  pl.BlockSpec((d, tf), lambda b, i, f: (0, nf + f)),
                  pl.BlockSpec((3, tf), lambda b, i, f: (0, f)),
                  pl.BlockSpec((3, tf), lambda b, i, f: (0, nf + f)),
                  pl.BlockSpec((1, tf), lambda b, i, f: (0, f)),
                  pl.BlockSpec((1, tf), lambda b, i, f: (0, nf + f)),
                  pl.BlockSpec((tf, d), lambda b, i, f: (f, 0)),
                  pl.BlockSpec((1, d), lambda b, i, f: (0, 0))],
        out_specs=pl.BlockSpec((1, tm, d), lambda b, i, f: (b, i, 0)),
        out_shape=jax.ShapeDtypeStruct((B, T, d), F32),
        scratch_shapes=[pltpu.VMEM((tm + 2 * edge, d), BF16), pltpu.VMEM((tm, d), F32)],
        compiler_params=_cparams("parallel", "parallel", "arbitrary"),
        name="conv_ffn",
    )(x, x, x, g_pre.reshape(1, d), mod, w_up, w_up, cw, cw, cb.reshape(1, -1), cb.reshape(1, -1), w_down, g_post.reshape(1, d))


def _prep_w_in(w_in):
    sizes = (512, 512, 512, 512, 16, 512, 128, 128, 512, 512, 512, 3072)
    offs = np.concatenate([[0], np.cumsum(sizes)])
    seg = lambda n: w_in[:, :, offs[n]:offs[n + 1]]
    mq, mk, mv, mo, mg, aq, ak, av, nq, nk, nv, zg = (seg(n) for n in range(12))
    mk = mk * (ML_HEAD_DIM ** -0.5)
    nq = nq * ATTN_SCALE
    main = jnp.concatenate([zg, mq, mk, mv, mo, aq, nq, nk, nv, ak, av], axis=-1).astype(BF16)
    gate = jnp.pad(mg, ((0, 0), (0, 0), (0, LANES - mg.shape[-1]))).astype(BF16)
    return main, gate


def kernel(x, c, ctx, c_ctx, w_mod, b_mod, g_pre_mix, g_post_mix, g_pre_ffn, g_post_ffn, w_in, b_ml_gates, g_ml_out, g_q, g_k, na_rpb, w_branch, w_out, w_up, conv_w, conv_b, w_down):
    B, T, d = x.shape
    Tc = ctx.shape[1]
    depth = w_mod.shape[0]
    rows = T // GRID_W

    w_main, w_gate = _prep_w_in(w_in)
    wb, wo, wu, wd = (a.astype(BF16) for a in (w_branch, w_out, w_up, w_down))
    n_mod = -(-(B + 1) // 8) * 8
    cc = jnp.concatenate([c, c_ctx[None], jnp.zeros((n_mod - B - 1, d), F32)], axis=0)
    mod_all = _mod_call(cc, w_mod, b_mod)
    tables = _rope_tables(T)
    bias_tables = _na_bias_tables(na_rpb, rows)
    gate_bias = jnp.pad(b_ml_gates, ((0, 0), (0, LANES - b_ml_gates.shape[-1])))
    gq2 = jnp.concatenate([g_q, g_q], axis=-1)
    gk2 = jnp.concatenate([g_k, g_k], axis=-1)
    zero_state = _zero_state(B)

    tm = min(TOKEN_TILE, T)
    tq = min(GQA_TQ, T)
    xc = ctx
    for l in range(depth):
        last = l == depth - 1
        mod_l = mod_all[l, :B].reshape(B, 6, d)
        mod_c = mod_all[l, B:B + 1].reshape(1, 6, d)
        bias_l = gate_bias[l][None]

        z, zg = _inproj(x, g_pre_mix[l], mod_l, w_main[l], w_gate[l], tm)
        zc, zgc = _inproj(xc.reshape(1, B * Tc, d), g_pre_mix[l], mod_c, w_main[l], w_gate[l], min(TOKEN_TILE, B * Tc))
        zc, zgc = zc.reshape(B, Tc, NZ), zgc.reshape(B, Tc, LANES)

        hfc, hbc, state = _mlstm(zc, zgc, bias_l, zero_state)
        hf, hb, _ = _mlstm(z, zg, bias_l, state)

        qp, kn = _qkprep(z, gq2[l][None], gk2[l][None], tables, tm)
        qpc, knc = _qkprep(zc, gq2[l][None], gk2[l][None], None, Tc)
        vc_a = jnp.concatenate([zc[:, :, COL_AV:COL_AV + LANES], jnp.ones((B, Tc, LANES), BF16)], axis=-1)
        v_a = jnp.concatenate([z[:, :, COL_AV:COL_AV + LANES], jnp.ones((B, T, LANES), BF16)], axis=-1)
        k_all = jnp.concatenate([knc, kn], axis=1)
        v_all = jnp.concatenate([vc_a, v_a], axis=1)
        y_ga = _gqa(qp, k_all, v_all, tq, GQA_KEY_CHUNK)
        y_na = _na(z, zc, bias_tables[l])

        x_mid = _merge(hf, hb, z, y_ga, y_na, x, g_ml_out[l], wb[l], wo[l], g_post_mix[l], mod_l, tm)
        x_new = _ffn(x_mid, g_pre_ffn[l], mod_l, wu[l], conv_w[l], conv_b[l], wd[l], g_post_ffn[l], min(FFN_TM, T), FFN_TF)

        if not last:
            yc_ga = _gqa(qpc, knc, vc_a, min(128, Tc), Tc)
            yc_na = _na_ctx(zc)
            flat = lambda a: a.reshape(1, B * Tc, a.shape[-1])
            xc_mid = _merge(flat(hfc), flat(hbc), flat(zc), flat(yc_ga), flat(yc_na), flat(xc), g_ml_out[l], wb[l], wo[l],
                            g_post_mix[l], mod_c, min(TOKEN_TILE, B * Tc)).reshape(B, Tc, d)
            xc = _ffn(xc_mid, g_pre_ffn[l], mod_c, wu[l], conv_w[l], conv_b[l], wd[l], g_post_ffn[l], Tc, FFN_TF)
        x = x_new
    return x
```

```python
import functools

import numpy as np
import jax
import jax.numpy as jnp
from jax import lax
from jax.experimental import pallas as pl
from jax.experimental.pallas import tpu as pltpu

F32 = jnp.float32
BF16 = jnp.bfloat16

D_MODEL = 1024
BRANCH_W = 512
HEAD_DIM = 64
ML_HEADS = 4
ML_HEAD_DIM = 128
GRID_W = 64
NA_WIN_R = 8
NA_WIN_C = 16
ROPE_THETA = 10000.0
FFN_DIM = 2816
EPS = 1e-6
LANES = 128
ML_CHUNK = 256
NA_ROWS = 4
GQA_KEY_CHUNK = 256
GQA_TQ = 512
TOKEN_TILE = 512
FFN_TM = 512
FFN_TF = FFN_DIM
NEG = -1e30
LOG2E = 1.4426950408889634
ATTN_SCALE = HEAD_DIM ** -0.5 * LOG2E
VMEM_LIMIT = 56 * 1024 * 1024

NZ = 7424
COL_ZG = 0
COL_MQ, COL_MK, COL_MV, COL_MO = 3072, 3584, 4096, 4608
COL_AQ = 5120
COL_NQ, COL_NK, COL_NV = 5632, 6144, 6656
COL_AK, COL_AV = 7168, 7296

_NT = (((1,), (1,)), ((), ()))
_TN = (((0,), (0,)), ((), ()))


def _cparams(*sem):
    return pltpu.CompilerParams(dimension_semantics=sem, vmem_limit_bytes=VMEM_LIMIT)


def _rms(x):
    return x * lax.rsqrt(jnp.mean(x * x, axis=-1, keepdims=True) + EPS)


def _sigmoid(x):
    return 1.0 / (1.0 + jnp.exp(-x))


def _mod_body(c_ref, w_ref, b_ref, o_ref):
    c = c_ref[...]
    s = (c * _sigmoid(c)).astype(BF16)
    o_ref[0] = jnp.dot(s, w_ref[0].astype(BF16), preferred_element_type=F32) + b_ref[0]


def _mod_call(cc, w_mod, b_mod):
    depth, d, n = w_mod.shape
    tn = 1536
    return pl.pallas_call(
        _mod_body,
        grid=(depth, n // tn),
        in_specs=[pl.BlockSpec((cc.shape[0], d), lambda l, j: (0, 0)),
                  pl.BlockSpec((1, d, tn), lambda l, j: (l, 0, j)),
                  pl.BlockSpec((1, 1, tn), lambda l, j: (l, 0, j))],
        out_specs=pl.BlockSpec((1, cc.shape[0], tn), lambda l, j: (l, 0, j)),
        out_shape=jax.ShapeDtypeStruct((depth, cc.shape[0], n), F32),
        compiler_params=_cparams("parallel", "parallel"),
        name="modulation",
    )(cc, w_mod, b_mod.reshape(depth, 1, n))


def _inproj_body(x_ref, g_ref, mod_ref, w_ref, wg_ref, z_ref, zg_ref):
    y = _rms(x_ref[0]) * g_ref[...]
    h = (y * (1.0 + mod_ref[0, 1:2, :]) + mod_ref[0, 0:1, :]).astype(BF16)
    z_ref[0] = jnp.dot(h, w_ref[...], preferred_element_type=F32).astype(BF16)
    first = pl.program_id(0) == 0

    @pl.when(first)
    def _():
        zg_ref[0, 0] = jnp.dot(h, wg_ref[...], preferred_element_type=F32)

    @pl.when(jnp.logical_not(first))
    def _():
        zg_ref[0, 0] = jnp.zeros(zg_ref.shape[2:], F32)


def _mod_index(mod):
    if mod.shape[0] > 1:
        return lambda b, *_: (b, 0, 0)
    return lambda b, *_: (0, 0, 0)


def _inproj(x, g, mod, w, wg, tm):
    B, T, d = x.shape
    tn = NZ // 2
    mod_index = _mod_index(mod)
    z, zg = pl.pallas_call(
        _inproj_body,
        grid=(NZ // tn, B, T // tm),
        in_specs=[pl.BlockSpec((1, tm, d), lambda j, b, i: (b, i, 0)),
                  pl.BlockSpec((1, d), lambda j, b, i: (0, 0)),
                  pl.BlockSpec((1, 6, d), lambda j, b, i: mod_index(b)),
                  pl.BlockSpec((d, tn), lambda j, b, i: (0, j)),
                  pl.BlockSpec((d, LANES), lambda j, b, i: (0, 0))],
        out_specs=[pl.BlockSpec((1, tm, tn), lambda j, b, i: (b, i, j)),
                   pl.BlockSpec((1, 1, tm, LANES), lambda j, b, i: (j, b, i, 0))],
        out_shape=[jax.ShapeDtypeStruct((B, T, NZ), BF16), jax.ShapeDtypeStruct((NZ // tn, B, T, LANES), F32)],
        compiler_params=_cparams("arbitrary", "arbitrary", "arbitrary"),
        name="inproj",
    )(x, g.reshape(1, d), mod, w, wg)
    return z, zg[0]


def _log_sigmoid(x):
    return jnp.minimum(x, 0.0) - jnp.log1p(jnp.exp(-jnp.abs(x)))


def _split_bf16(a, parts):
    out = []
    r = a
    for _ in range(parts):
        p = r.astype(BF16)
        out.append(p)
        r = r - p.astype(F32)
    return out


def _twice(a):
    return jnp.concatenate([a, a], axis=-1)


def _mlstm_chunk(q, k, v1, b_rep, ig_rep, b_row, ig_row, b_last, mask, c_ext, m):
    s0 = lax.dot_general(q, k, _NT, preferred_element_type=F32)
    logd = jnp.where(mask, _twice(b_rep) - (b_row - ig_row), NEG)
    bm = b_rep + m
    m_t = jnp.maximum(bm, jnp.max(logd, axis=-1, keepdims=True))
    s = s0 * jnp.exp(logd - _twice(m_t))
    inter = jnp.exp(bm - m_t)
    q_c = jnp.dot(q, c_ext.astype(BF16), preferred_element_type=F32)
    tot = jnp.dot(s.astype(BF16), v1, preferred_element_type=F32) + _twice(inter) * q_c
    h = tot[:, :ML_HEAD_DIM] / jnp.maximum(jnp.abs(tot[:, ML_HEAD_DIM:]), jnp.exp(-m_t))

    lw = b_last - b_rep + ig_rep
    m_new = jnp.maximum(b_last + m, jnp.max(lw, axis=0, keepdims=True))
    w = jnp.exp(lw - m_new)
    decay = jnp.exp(b_last + m - m_new)
    wv = (_twice(w) * v1.astype(F32)).astype(BF16)
    c_new = _twice(decay) * c_ext + lax.dot_general(k, wv, _TN, preferred_element_type=F32)
    return h, c_new, m_new


def _mlstm_body(qf_ref, kf_ref, vf_ref, gf_ref, qb_ref, kb_ref, vb_ref, gb_ref, bias_ref,
                c0f_ref, m0f_ref, c0b_ref, m0b_ref,
                hf_ref, hb_ref, c1f_ref, m1f_ref, c1b_ref, m1b_ref,
                cf_scr, mf_scr, cb_scr, mb_scr):
    i = pl.program_id(1)
    L = ML_CHUNK
    H, dh = ML_HEADS, ML_HEAD_DIM

    @pl.when(i == 0)
    def _():
        cf_scr[...] = c0f_ref[0]
        mf_scr[...] = m0f_ref[0]
        cb_scr[...] = c0b_ref[0]
        mb_scr[...] = m0b_ref[0]

    row = lax.broadcasted_iota(jnp.int32, (L, L), 0)
    col = lax.broadcasted_iota(jnp.int32, (L, L), 1)
    lane = lax.broadcasted_iota(jnp.int32, (L, LANES), 1)
    is_forget = (lane % 8) >= 4
    eye = jnp.where(row == col, 1.0, 0.0).astype(BF16)
    ones = jnp.ones((L, dh), BF16)

    def replicate(mat, rows):
        rhs = jnp.concatenate([jnp.broadcast_to(r, (LANES, L)) for r in rows], axis=0)
        return sum(lax.dot_general(mat, p, _NT, preferred_element_type=F32) for p in _split_bf16(rhs, 2))

    def run(fwd, q_ref, k_ref, v_ref, g_ref, h_ref, c_scr, m_scr):
        mask = (row >= col) if fwd else (row <= col)
        tri = jnp.where(mask, 1.0, 0.0).astype(BF16)
        g = g_ref[0] + bias_ref[...]
        G = jnp.where(is_forget, _log_sigmoid(g), g)
        r0 = 0 if fwd else 2 * H
        R = G.T[r0:r0 + 2 * H]
        Bsum = sum(lax.dot_general(p, tri, _NT, preferred_element_type=F32) for p in _split_bf16(R, 3))
        for pair in range(H // 2):
            heads = (2 * pair, 2 * pair + 1)
            b_rep2 = replicate(tri, [R[H + hh:H + hh + 1] for hh in heads])
            ig_rep2 = replicate(eye, [R[hh:hh + 1] for hh in heads])
            for j, hh in enumerate(heads):
                b_rep = b_rep2[:, j * LANES:(j + 1) * LANES]
                ig_rep = ig_rep2[:, j * LANES:(j + 1) * LANES]
                b_last = b_rep[L - 1:L, :] if fwd else b_rep[0:1, :]
                sl = slice(hh * dh, (hh + 1) * dh)
                v1 = jnp.concatenate([v_ref[0, :, sl], ones], axis=-1)
                h, c_new, m_new = _mlstm_chunk(
                    q_ref[0, :, sl], k_ref[0, :, sl], v1, b_rep, ig_rep,
                    Bsum[H + hh:H + hh + 1], R[hh:hh + 1], b_last, mask, c_scr[hh], m_scr[hh])
                h_ref[0, :, sl] = h.astype(h_ref.dtype)
                c_scr[hh] = c_new
                m_scr[hh] = m_new

    run(True, qf_ref, kf_ref, vf_ref, gf_ref, hf_ref, cf_scr, mf_scr)
    run(False, qb_ref, kb_ref, vb_ref, gb_ref, hb_ref, cb_scr, mb_scr)

    @pl.when(i == pl.num_programs(1) - 1)
    def _():
        c1f_ref[0] = cf_scr[...]
        m1f_ref[0] = mf_scr[...]
        c1b_ref[0] = cb_scr[...]
        m1b_ref[0] = mb_scr[...]


def _zero_state(B):
    H, dh = ML_HEADS, ML_HEAD_DIM
    return (jnp.zeros((B, H, dh, 2 * dh), F32), jnp.zeros((B, H, 1, LANES), F32)) * 2


def _mlstm(z, zg, bias, state):
    B, T, _ = z.shape
    L = ML_CHUNK
    nc = T // L
    H, dh = ML_HEADS, ML_HEAD_DIM
    bq, bk, bv = COL_MQ // BRANCH_W, COL_MK // BRANCH_W, COL_MV // BRANCH_W
    fw = lambda blk: (lambda b, i: (b, i, blk))
    bw = lambda blk: (lambda b, i: (b, nc - 1 - i, blk))
    st4 = lambda b, i: (b, 0, 0, 0)
    c_spec = pl.BlockSpec((1, H, dh, 2 * dh), st4)
    m_spec = pl.BlockSpec((1, H, 1, LANES), st4)
    c_shape = jax.ShapeDtypeStruct((B, H, dh, 2 * dh), F32)
    m_shape = jax.ShapeDtypeStruct((B, H, 1, LANES), F32)
    tok = lambda idx: pl.BlockSpec((1, L, BRANCH_W), idx)
    gsp = lambda idx: pl.BlockSpec((1, L, LANES), idx)
    outs = pl.pallas_call(
        _mlstm_body,
        grid=(B, nc),
        in_specs=[tok(fw(bq)), tok(fw(bk)), tok(fw(bv)), gsp(fw(0)),
                  tok(bw(bq)), tok(bw(bk)), tok(bw(bv)), gsp(bw(0)),
                  pl.BlockSpec((1, LANES), lambda b, i: (0, 0)),
                  c_spec, m_spec, c_spec, m_spec],
        out_specs=[tok(fw(0)), tok(bw(0)), c_spec, m_spec, c_spec, m_spec],
        out_shape=[jax.ShapeDtypeStruct((B, T, BRANCH_W), BF16)] * 2 + [c_shape, m_shape] * 2,
        scratch_shapes=[pltpu.VMEM((H, dh, 2 * dh), F32), pltpu.VMEM((H, 1, LANES), F32)] * 2,
        compiler_params=_cparams("parallel", "arbitrary"),
        name="mlstm_scan",
    )(z, z, z, zg, z, z, z, zg, bias, *state)
    return outs[0], outs[1], tuple(outs[2:])


def _qkprep_body(rope, q_ref, k_ref, gq_ref, gk_ref, *rest):
    if rope:
        cos_ref, sa_ref, sb_ref, qo_ref, ko_ref = rest
    else:
        qo_ref, ko_ref = rest
    tm = q_ref.shape[1]
    r = lax.broadcasted_iota(jnp.int32, (LANES, LANES), 0) // HEAD_DIM
    c = lax.broadcasted_iota(jnp.int32, (LANES, LANES), 1) // HEAD_DIM
    head_ones = jnp.where(r == c, 1.0, 0.0).astype(BF16)
    lo = lax.broadcasted_iota(jnp.int32, (tm, LANES), 1) < HEAD_DIM

    def norm_rope(x, g):
        ss = jnp.dot((x * x).astype(BF16), head_ones, preferred_element_type=F32)
        y = x * lax.rsqrt(ss * (1.0 / HEAD_DIM) + EPS) * g
        if rope:
            y = y * cos_ref[...] + pltpu.roll(y, LANES - 16, 1) * sa_ref[...] + pltpu.roll(y, 16, 1) * sb_ref[...]
        return y

    for j in range(BRANCH_W // LANES):
        y = norm_rope(q_ref[0, :, j * LANES:(j + 1) * LANES].astype(F32), gq_ref[...]) * ATTN_SCALE
        even = jnp.where(lo, y, 0.0)
        odd = jnp.where(lo, 0.0, y)
        if j // 2 == 0:
            odd = pltpu.roll(odd, HEAD_DIM, 1)
        else:
            even = pltpu.roll(even, HEAD_DIM, 1)
        qo_ref[0, :, (2 * j) * LANES:(2 * j + 1) * LANES] = even.astype(BF16)
        qo_ref[0, :, (2 * j + 1) * LANES:(2 * j + 2) * LANES] = odd.astype(BF16)
    ko_ref[0] = norm_rope(k_ref[0].astype(F32), gk_ref[...]).astype(BF16)


def _qkprep(z, gq, gk, tables, tm):
    B, T, _ = z.shape
    rope = tables is not None
    in_specs = [pl.BlockSpec((1, tm, BRANCH_W), lambda b, i: (b, i, COL_AQ // BRANCH_W)),
                pl.BlockSpec((1, tm, LANES), lambda b, i: (b, i, COL_AK // LANES)),
                pl.BlockSpec((1, LANES), lambda b, i: (0, 0)),
                pl.BlockSpec((1, LANES), lambda b, i: (0, 0))]
    args = [z, z, gq, gk]
    if rope:
        in_specs += [pl.BlockSpec((tm, LANES), lambda b, i: (i, 0))] * 3
        args += list(tables)
    return pl.pallas_call(
        functools.partial(_qkprep_body, rope),
        grid=(B, T // tm),
        in_specs=in_specs,
        out_specs=[pl.BlockSpec((1, tm, 2 * BRANCH_W), lambda b, i: (b, i, 0)),
                   pl.BlockSpec((1, tm, LANES), lambda b, i: (b, i, 0))],
        out_shape=[jax.ShapeDtypeStruct((B, T, 2 * BRANCH_W), BF16), jax.ShapeDtypeStruct((B, T, LANES), BF16)],
        compiler_params=_cparams("parallel", "parallel"),
        name="qk_prep",
    )(*args)


def _rope_tables(n_tok):
    half = HEAD_DIM // 2
    t = jnp.arange(n_tok)
    inv = ROPE_THETA ** (-jnp.arange(0, half, 2, dtype=F32) / half)
    ang_r = (t // GRID_W).astype(F32)[:, None] * inv
    ang_c = (t % GRID_W).astype(F32)[:, None] * inv
    ang = jnp.concatenate([ang_r, ang_r, ang_c, ang_c], axis=-1)
    cos, sin = jnp.cos(ang), jnp.sin(ang)
    even_quarter = ((np.arange(HEAD_DIM) // (HEAD_DIM // 4)) % 2 == 0)[None, :]
    sa = jnp.where(even_quarter, -sin, 0.0)
    sb = jnp.where(even_quarter, 0.0, sin)
    tile = lambda a: jnp.concatenate([a, a], axis=-1)
    return tile(cos), tile(sa), tile(sb)


def _gqa_body(tk, q_ref, k_ref, v_ref, o_ref):
    g = pl.program_id(1)
    tq = q_ref.shape[1]
    S = k_ref.shape[1]
    n_heads = BRANCH_W // LANES
    q = jnp.concatenate([q_ref[0, :, h * LANES:(h + 1) * LANES] for h in range(n_heads)], axis=0)
    bounds = [0] + list(range(S % tk, S, tk)) if S % tk else list(range(0, S, tk))
    m = acc = None
    for off, end in zip(bounds, bounds[1:] + [S]):
        k = k_ref[0, off:end, :]
        v = v_ref[0, off:end, :]
        s = lax.dot_general(q, k, _NT, preferred_element_type=F32)
        mc = jnp.max(s, axis=-1, keepdims=True)
        if m is None:
            m = mc
            acc = jnp.dot(jnp.exp2(s - m).astype(BF16), v, preferred_element_type=F32)
        else:
            m_new = jnp.maximum(m, mc)
            acc = jnp.exp2(m - m_new) * acc + jnp.dot(jnp.exp2(s - m_new).astype(BF16), v, preferred_element_type=F32)
            m = m_new
    o = acc[:, :LANES] / acc[:, LANES:]
    lo = lax.broadcasted_iota(jnp.int32, (tq, LANES), 1) < HEAD_DIM
    first = g == 0
    for j in range(n_heads // 2):
        a = o[(2 * j) * tq:(2 * j + 1) * tq]
        b = o[(2 * j + 1) * tq:(2 * j + 2) * tq]
        pair0 = jnp.where(lo, a, pltpu.roll(b, HEAD_DIM, 1))
        pair1 = jnp.where(lo, pltpu.roll(a, HEAD_DIM, 1), b)
        o_ref[0, :, j * LANES:(j + 1) * LANES] = jnp.where(first, pair0, pair1).astype(BF16)


def _gqa(qp, k_all, v_all, tq, tk):
    B, T, _ = qp.shape
    S = k_all.shape[1]
    n_kv = 2
    return pl.pallas_call(
        functools.partial(_gqa_body, tk),
        grid=(B, n_kv, T // tq),
        in_specs=[pl.BlockSpec((1, tq, BRANCH_W), lambda b, g, i: (b, i, g)),
                  pl.BlockSpec((1, S, LANES), lambda b, g, i: (b, 0, 0)),
                  pl.BlockSpec((1, S, 2 * LANES), lambda b, g, i: (b, 0, 0))],
        out_specs=pl.BlockSpec((1, tq, BRANCH_W // n_kv), lambda b, g, i: (b, i, g)),
        out_shape=jax.ShapeDtypeStruct((B, T, BRANCH_W), BF16),
        compiler_params=_cparams("parallel", "parallel", "parallel"),
        name="gqa_flash",
    )(qp, k_all, v_all)


def _na_body(has_win, wk, rows, q_ref, *rest):
    if has_win:
        k_ref, v_ref, kc_ref, vc_ref, bm_ref, o_ref = rest
    else:
        kc_ref, vc_ref, o_ref = rest
    i = pl.program_id(1)
    tq = q_ref.shape[1]
    lo = lax.broadcasted_iota(jnp.int32, (tq, LANES), 1) < HEAD_DIM
    if has_win:
        ks = jnp.clip(i * NA_ROWS - NA_WIN_R // 2, 0, rows - wk // GRID_W)
        off = pl.multiple_of(ks * GRID_W, GRID_W)
    for j in range(BRANCH_W // LANES):
        sl = slice(j * LANES, (j + 1) * LANES)
        q2 = q_ref[0, :, sl]
        kc = kc_ref[0, :, sl]
        vc = jnp.concatenate([vc_ref[0, :, sl], jnp.ones((kc.shape[0], LANES), BF16)], axis=-1)
        if has_win:
            kw = k_ref[0, pl.ds(off, wk), sl]
            vw = jnp.concatenate([v_ref[0, pl.ds(off, wk), sl], jnp.ones((wk, LANES), BF16)], axis=-1)
        halves = []
        for e in range(2):
            qh = jnp.where(lo if e == 0 else jnp.logical_not(lo), q2, jnp.zeros_like(q2))
            sc = lax.dot_general(qh, kc, _NT, preferred_element_type=F32)
            m = jnp.max(sc, axis=-1, keepdims=True)
            if has_win:
                sw = lax.dot_general(qh, kw, _NT, preferred_element_type=F32) + bm_ref[0, 2 * j + e]
                m = jnp.maximum(m, jnp.max(sw, axis=-1, keepdims=True))
            acc = jnp.dot(jnp.exp2(sc - m).astype(BF16), vc, preferred_element_type=F32)
            if has_win:
                acc = acc + jnp.dot(jnp.exp2(sw - m).astype(BF16), vw, preferred_element_type=F32)
            halves.append(acc[:, :LANES] / acc[:, LANES:])
        o_ref[0, :, sl] = jnp.where(lo, halves[0], halves[1]).astype(BF16)


def _na(z, zc, bm):
    B, T, _ = z.shape
    Tc = zc.shape[1]
    rows = T // GRID_W
    tq = NA_ROWS * GRID_W
    wk = (NA_ROWS + NA_WIN_R - 1) * GRID_W
    nb = rows // NA_ROWS
    bq, bk, bv = COL_NQ // BRANCH_W, COL_NK // BRANCH_W, COL_NV // BRANCH_W

    def bm_index(b, i):
        return (jnp.where(i == 0, 0, jnp.where(i == nb - 1, 2, 1)), 0, 0, 0)

    return pl.pallas_call(
        functools.partial(_na_body, True, wk, rows),
        grid=(B, nb),
        in_specs=[pl.BlockSpec((1, tq, BRANCH_W), lambda b, i: (b, i, bq)),
                  pl.BlockSpec((1, T, BRANCH_W), lambda b, i: (b, 0, bk)),
                  pl.BlockSpec((1, T, BRANCH_W), lambda b, i: (b, 0, bv)),
                  pl.BlockSpec((1, Tc, BRANCH_W), lambda b, i: (b, 0, bk)),
                  pl.BlockSpec((1, Tc, BRANCH_W), lambda b, i: (b, 0, bv)),
                  pl.BlockSpec((1, 2 * BRANCH_W // LANES, tq, wk), bm_index)],
        out_specs=pl.BlockSpec((1, tq, BRANCH_W), lambda b, i: (b, i, 0)),
        out_shape=jax.ShapeDtypeStruct((B, T, BRANCH_W), BF16),
        compiler_params=_cparams("parallel", "arbitrary"),
        name="na_window",
    )(z, z, z, zc, zc, bm)


def _na_ctx(zc):
    B, Tc, _ = zc.shape
    bq, bk, bv = COL_NQ // BRANCH_W, COL_NK // BRANCH_W, COL_NV // BRANCH_W
    return pl.pallas_call(
        functools.partial(_na_body, False, 0, 0),
        grid=(B, 1),
        in_specs=[pl.BlockSpec((1, Tc, BRANCH_W), lambda b, i: (b, 0, bq)),
                  pl.BlockSpec((1, Tc, BRANCH_W), lambda b, i: (b, 0, bk)),
                  pl.BlockSpec((1, Tc, BRANCH_W), lambda b, i: (b, 0, bv))],
        out_specs=pl.BlockSpec((1, Tc, BRANCH_W), lambda b, i: (b, 0, 0)),
        out_shape=jax.ShapeDtypeStruct((B, Tc, BRANCH_W), BF16),
        compiler_params=_cparams("parallel", "arbitrary"),
        name="na_context",
    )(zc, zc, zc)


def _na_bias_tables(rpb, rows):
    R, W = NA_ROWS, GRID_W
    wr, wc = min(NA_WIN_R, rows), NA_WIN_C
    nkr = R + NA_WIN_R - 1
    nb = rows // R
    depth, heads = rpb.shape[:2]
    c = np.arange(W)[:, None]
    kc = np.arange(W)[None, :]
    c0 = np.clip(c - wc // 2, 0, W - wc)
    col_ok = (kc >= c0) & (kc < c0 + wc)
    onehot = ((kc - c + (NA_WIN_C - 1))[None] == np.arange(2 * NA_WIN_C - 1)[:, None, None]) & col_ok[None]
    toep = jnp.einsum('lhrd,dcx->lhrcx', rpb * LOG2E, jnp.asarray(onehot, F32), precision=lax.Precision.HIGHEST)
    toep = jnp.where(col_ok, toep, NEG)
    masked = jnp.full((depth, heads, W, W), NEG, F32)
    kinds = []
    for blk in (0, 1, nb - 1):
        rb = blk * R
        ks = int(np.clip(rb - NA_WIN_R // 2, 0, rows - nkr))
        q_rows = []
        for ri in range(R):
            r = rb + ri
            r0 = int(np.clip(r - wr // 2, 0, rows - wr))
            tiles = [toep[:, :, a - r + (NA_WIN_R - 1)] if r0 <= a < r0 + wr else masked for a in range(ks, ks + nkr)]
            q_rows.append(jnp.concatenate(tiles, axis=-1))
        kinds.append(jnp.concatenate(q_rows, axis=-2))
    return jnp.stack(kinds, axis=1)


def _merge_body(hf_ref, hb_ref, zo_ref, zg_ref, ga_ref, na_ref, x_ref, gml_ref, wb_ref, wo_ref, gpost_ref, mod_ref, o_ref):
    h = hf_ref[0].astype(F32) + hb_ref[0].astype(F32)
    hn = jnp.concatenate([_rms(h[:, t * ML_HEAD_DIM:(t + 1) * ML_HEAD_DIM]) for t in range(ML_HEADS)], axis=-1)
    y_ml = (hn * gml_ref[...] * _sigmoid(zo_ref[0].astype(F32))).astype(BF16)
    acc = None
    for n, y in enumerate((y_ml, ga_ref[0], na_ref[0])):
        gate = _sigmoid(zg_ref[0, :, n * D_MODEL:(n + 1) * D_MODEL].astype(F32))
        term = gate * jnp.dot(y, wb_ref[n], preferred_element_type=F32)
        acc = term if acc is None else acc + term
    o = jnp.dot(acc.astype(BF16), wo_ref[...], preferred_element_type=F32)
    o_ref[0] = x_ref[0] + mod_ref[0, 2:3, :] * (_rms(o) * gpost_ref[...])


def _merge(hf, hb, z, y_ga, y_na, x, g_ml, wb, wo, g_post, mod, tm):
    B, T, d = x.shape
    tok = lambda w, blk: pl.BlockSpec((1, tm, w), lambda b, i: (b, i, blk))
    full = lambda a: pl.BlockSpec(a.shape, lambda b, i: (0,) * a.ndim)
    g_ml = g_ml.reshape(1, BRANCH_W)
    g_post = g_post.reshape(1, d)
    return pl.pallas_call(
        _merge_body,
        grid=(B, T // tm),
        in_specs=[tok(BRANCH_W, 0), tok(BRANCH_W, 0), tok(BRANCH_W, COL_MO // BRANCH_W), tok(3 * d, 0),
                  tok(BRANCH_W, 0), tok(BRANCH_W, 0), tok(d, 0),
                  full(g_ml), full(wb), full(wo), full(g_post),
                  pl.BlockSpec((1, 6, d), _mod_index(mod))],
        out_specs=tok(d, 0),
        out_shape=jax.ShapeDtypeStruct((B, T, d), F32),
        compiler_params=_cparams("parallel", "parallel"),
        name="merge_out",
    )(hf, hb, z, z, y_ga, y_na, x, g_ml, wb, wo, g_post, mod)


def _ffn_body(x_ref, xp_ref, xn_ref, g_ref, mod_ref, wa_ref, wg_ref, cwa_ref, cwg_ref, cba_ref, cbg_ref, wd_ref, gpost_ref,
              o_ref, h_scr, acc_scr):
    i = pl.program_id(1)
    f = pl.program_id(2)
    tm = x_ref.shape[1]
    edge = xp_ref.shape[1]
    n_rows = tm + 2 * edge

    @pl.when(f == 0)
    def _():
        def nm(x):
            return _rms(x) * g_ref[...] * (1.0 + mod_ref[0, 4:5, :]) + mod_ref[0, 3:4, :]
        has_prev = (i > 0).astype(F32)
        has_next = (i < pl.num_programs(1) - 1).astype(F32)
        h_scr[0:edge] = (nm(xp_ref[0]) * has_prev).astype(BF16)
        h_scr[edge:edge + tm] = nm(x_ref[0]).astype(BF16)
        h_scr[edge + tm:n_rows] = (nm(xn_ref[0]) * has_next).astype(BF16)
        acc_scr[...] = jnp.zeros(acc_scr.shape, F32)

    h2 = h_scr[...]

    def conv_branch(w_ref, cw_ref, cb_ref):
        u = jnp.dot(h2, w_ref[...], preferred_element_type=F32)
        u_prev = pltpu.roll(u, 1, 0)[edge:edge + tm]
        u_next = pltpu.roll(u, n_rows - 1, 0)[edge:edge + tm]
        return u_prev * cw_ref[0:1, :] + u[edge:edge + tm] * cw_ref[1:2, :] + u_next * cw_ref[2:3, :] + cb_ref[...]

    a = conv_branch(wa_ref, cwa_ref, cba_ref)
    g = conv_branch(wg_ref, cwg_ref, cbg_ref)
    act = (a * (g * _sigmoid(g))).astype(BF16)
    acc_scr[...] += jnp.dot(act, wd_ref[...], preferred_element_type=F32)

    @pl.when(f == pl.num_programs(2) - 1)
    def _():
        o_ref[0] = x_ref[0] + mod_ref[0, 5:6, :] * (_rms(acc_scr[...]) * gpost_ref[...])


def _ffn(x, g_pre, mod, w_up, cw, cb, w_down, g_post, tm, tf):
    B, T, d = x.shape
    edge = 8
    nf = FFN_DIM // tf
    per = tm // edge
    last_edge = T // edge - 1
    return pl.pallas_call(
        _ffn_body,
        grid=(B, T // tm, nf),
        in_specs=[pl.BlockSpec((1, tm, d), lambda b, i, f: (b, i, 0)),
                  pl.BlockSpec((1, edge, d), lambda b, i, f: (b, jnp.maximum(i * per - 1, 0), 0)),
                  pl.BlockSpec((1, edge, d), lambda b, i, f: (b, jnp.minimum((i + 1) * per, last_edge), 0)),
                  pl.BlockSpec((1, d), lambda b, i, f: (0, 0)),
                  pl.BlockSpec((1, 6, d), _mod_index(mod)),
                  pl.BlockSpec((d, tf), lambda b, i, f: (0, f)),
                  pl.BlockSpec((d, tf), lambda b, i, f: (0, nf + f)),
                  pl.BlockSpec((3, tf), lambda b, i, f: (0, f)),
                  pl.BlockSpec((3, tf), lambda b, i, f: (0, nf + f)),
                  pl.BlockSpec((1, tf), lambda b, i, f: (0, f)),
                  pl.BlockSpec((1, tf), lambda b, i, f: (0, nf + f)),
                  pl.BlockSpec((tf, d), lambda b, i, f: (f, 0)),
                  pl.BlockSpec((1, d), lambda b, i, f: (0, 0))],
        out_specs=pl.BlockSpec((1, tm, d), lambda b, i, f: (b, i, 0)),
        out_shape=jax.ShapeDtypeStruct((B, T, d), F32),
        scratch_shapes=[pltpu.VMEM((tm + 2 * edge, d), BF16), pltpu.VMEM((tm, d), F32)],
        compiler_params=_cparams("parallel", "parallel", "arbitrary"),
        name="conv_ffn",
    )(x, x, x, g_pre.reshape(1, d), mod, w_up, w_up, cw, cw, cb.reshape(1, -1), cb.reshape(1, -1), w_down, g_post.reshape(1, d))


def _prep_w_in(w_in):
    sizes = (512, 512, 512, 512, 16, 512, 128, 128, 512, 512, 512, 3072)
    offs = np.concatenate([[0], np.cumsum(sizes)])
    col_scale = np.ones(offs[-1], np.float32)
    col_scale[offs[1]:offs[2]] = ML_HEAD_DIM ** -0.5
    col_scale[offs[8]:offs[9]] = ATTN_SCALE
    w16 = (w_in * col_scale).astype(BF16)
    mq, mk, mv, mo, mg, aq, ak, av, nq, nk, nv, zg = (w16[:, :, offs[n]:offs[n + 1]] for n in range(12))
    main = jnp.concatenate([zg, mq, mk, mv, mo, aq, nq, nk, nv, ak, av], axis=-1)
    gate = jnp.pad(mg, ((0, 0), (0, 0), (0, LANES - mg.shape[-1])))
    return main, gate


def kernel(x, c, ctx, c_ctx, w_mod, b_mod, g_pre_mix, g_post_mix, g_pre_ffn, g_post_ffn, w_in, b_ml_gates, g_ml_out, g_q, g_k, na_rpb, w_branch, w_out, w_up, conv_w, conv_b, w_down):
    B, T, d = x.shape
    Tc = ctx.shape[1]
    depth = w_mod.shape[0]
    rows = T // GRID_W

    w_main, w_gate = _prep_w_in(w_in)
    wb, wo, wu, wd = (a.astype(BF16) for a in (w_branch, w_out, w_up, w_down))
    n_mod = -(-(B + 1) // 8) * 8
    cc = jnp.concatenate([c, c_ctx[None], jnp.zeros((n_mod - B - 1, d), F32)], axis=0)
    mod_all = _mod_call(cc, w_mod, b_mod)
    tables = _rope_tables(T)
    bias_tables = _na_bias_tables(na_rpb, rows)
    gate_bias = jnp.pad(b_ml_gates, ((0, 0), (0, LANES - b_ml_gates.shape[-1])))
    gq2 = jnp.concatenate([g_q, g_q], axis=-1)
    gk2 = jnp.concatenate([g_k, g_k], axis=-1)
    zero_state = _zero_state(B)

    tm = min(TOKEN_TILE, T)
    tq = min(GQA_TQ, T)
    xc = ctx
    for l in range(depth):
        last = l == depth - 1
        mod_l = mod_all[l, :B].reshape(B, 6, d)
        mod_c = mod_all[l, B:B + 1].reshape(1, 6, d)
        bias_l = gate_bias[l][None]

        z, zg = _inproj(x, g_pre_mix[l], mod_l, w_main[l], w_gate[l], tm)
        zc, zgc = _inproj(xc.reshape(1, B * Tc, d), g_pre_mix[l], mod_c, w_main[l], w_gate[l], min(TOKEN_TILE, B * Tc))
        zc, zgc = zc.reshape(B, Tc, NZ), zgc.reshape(B, Tc, LANES)

        hfc, hbc, state = _mlstm(zc, zgc, bias_l, zero_state)
        hf, hb, _ = _mlstm(z, zg, bias_l, state)

        qp, kn = _qkprep(z, gq2[l][None], gk2[l][None], tables, tm)
        qpc, knc = _qkprep(zc, gq2[l][None], gk2[l][None], None, Tc)
        vc_a = jnp.concatenate([zc[:, :, COL_AV:COL_AV + LANES], jnp.ones((B, Tc, LANES), BF16)], axis=-1)
        v_a = jnp.concatenate([z[:, :, COL_AV:COL_AV + LANES], jnp.ones((B, T, LANES), BF16)], axis=-1)
        k_all = jnp.concatenate([knc, kn], axis=1)
        v_all = jnp.concatenate([vc_a, v_a], axis=1)
        y_ga = _gqa(qp, k_all, v_all, tq, GQA_KEY_CHUNK)
        y_na = _na(z, zc, bias_tables[l])

        x_mid = _merge(hf, hb, z, y_ga, y_na, x, g_ml_out[l], wb[l], wo[l], g_post_mix[l], mod_l, tm)
        x_new = _ffn(x_mid, g_pre_ffn[l], mod_l, wu[l], conv_w[l], conv_b[l], wd[l], g_post_ffn[l], min(FFN_TM, T), FFN_TF)

        if not last:
            yc_ga = _gqa(qpc, knc, vc_a, min(GQA_TQ, Tc), Tc)
            yc_na = _na_ctx(zc)
            flat = lambda a: a.reshape(1, B * Tc, a.shape[-1])
            xc_mid = _merge(flat(hfc), flat(hbc), flat(zc), flat(yc_ga), flat(yc_na), flat(xc), g_ml_out[l], wb[l], wo[l],
                            g_post_mix[l], mod_c, min(TOKEN_TILE, B * Tc)).reshape(B, Tc, d)
            xc = _ffn(xc_mid, g_pre_ffn[l], mod_c, wu[l], conv_w[l], conv_b[l], wd[l], g_post_ffn[l], Tc, FFN_TF)
        x = x_new
    return x
```
